```python
import jax, jax.numpy as jnp
from jax import lax
import numpy as np

D_MODEL = 4096
BATCH = 1
SEQ = 16384
DEPTH = 4
DEC_BATCH = 8
DEC_SEQ = 16
PAST_LEN = 2048

CHUNK = 64
HEAD_DIM = 128
N_HEADS = D_MODEL // HEAD_DIM
HA = N_HEADS // 2
KV_A = max(HA // 4, 1)
G_A = HA // KV_A
HB = N_HEADS - HA
HC = N_HEADS // 2
WINDOW = 128
A_BAND_CHUNKS = WINDOW // CHUNK + 1
A_REACH = (A_BAND_CHUNKS - 1) * CHUNK
B_BAND_CHUNKS = 9
B_REACH = (B_BAND_CHUNKS - 1) * CHUNK
REL_CLIP = 256
N_REL = 2 * REL_CLIP + 1
SB_BLOCK = 128
ROPE_THETA = 10000.0
D_FF = ((8 * D_MODEL + 3 * 256 - 1) // (3 * 256)) * 256
RMS_EPS = 1e-6
NEG_INF = -1e30
N_AB = (DEPTH + 1) // 2
N_C = DEPTH // 2
_QA = HA * HEAD_DIM
_KA = KV_A * HEAD_DIM
_QB = HB * HEAD_DIM
AB_SPLITS = (_QA, _QA + _KA, _QA + 2 * _KA, _QA + 2 * _KA + _QB, _QA + 2 * _KA + 2 * _QB)
AB_WIDTH = _QA + 2 * _KA + 3 * _QB

kernel_name = 'hybrid_chunk_stream_encoder_step'


def rmsnorm(x, g):
    xf = x.astype(jnp.float32)
    y = xf * lax.rsqrt(jnp.mean(xf * xf, axis=-1, keepdims=True) + RMS_EPS)
    return (y * g.astype(jnp.float32)).astype(x.dtype)


def swiglu(h, w_gate, w_up, w_down):
    return (jax.nn.silu(h @ w_gate) * (h @ w_up)) @ w_down


def rope(x, pos):
    half = x.shape[-1] // 2
    inv = ROPE_THETA ** (-jnp.arange(half, dtype=jnp.float32) / half)
    ang = pos.astype(jnp.float32)[:, None] * inv[None, :]
    cos = jnp.cos(ang)[:, None, :]
    sin = jnp.sin(ang)[:, None, :]
    xf = x.astype(jnp.float32)
    x1, x2 = xf[..., :half], xf[..., half:]
    return jnp.concatenate([x1 * cos - x2 * sin, x2 * cos + x1 * sin], axis=-1).astype(x.dtype)


def band_attend(q, k, v, q_pos, k_pos, bias_table, sink):
    hk, g, d = q.shape[2], q.shape[3], q.shape[4]
    logits = jnp.einsum('bqhgd,bkhd->bhgqk', q.astype(jnp.float32), k.astype(jnp.float32)) * (d ** -0.5)
    if bias_table is not None:
        rel = jnp.clip(q_pos[:, None] - k_pos[None, :], -REL_CLIP, REL_CLIP) + REL_CLIP
        bias = bias_table.astype(jnp.float32)[:, rel]
        logits = logits + bias.reshape(hk, g, q_pos.shape[0], k_pos.shape[0])
    logits = jnp.where(k_pos >= 0, logits, NEG_INF)
    if sink is not None:
        s = jnp.broadcast_to(sink.astype(jnp.float32).reshape(hk, g, 1, 1), logits.shape[:-1] + (1,))
        p = jax.nn.softmax(jnp.concatenate([logits, s], axis=-1), axis=-1)[..., :-1]
    else:
        p = jax.nn.softmax(logits, axis=-1)
    out = jnp.einsum('bhgqk,bkhd->bqhgd', p, v.astype(jnp.float32))
    return out.astype(v.dtype)


def band_prompt(q, k, v, n_band, bias_table, sink):
    bn, s = q.shape[0], q.shape[1]
    nc = s // CHUNK
    pad = (n_band - 1) * CHUNK
    span = n_band * CHUNK
    k_pad = jnp.pad(k, ((0, 0), (pad, 0), (0, 0), (0, 0)))
    v_pad = jnp.pad(v, ((0, 0), (pad, 0), (0, 0), (0, 0)))
    q_blocks = jnp.moveaxis(q.reshape((bn, nc, CHUNK) + q.shape[2:]), 1, 0)

    def one(args):
        c, q_blk = args
        start = c * CHUNK
        k_blk = lax.dynamic_slice_in_dim(k_pad, start, span, axis=1)
        v_blk = lax.dynamic_slice_in_dim(v_pad, start, span, axis=1)
        q_pos = start + jnp.arange(CHUNK)
        k_pos = start - pad + jnp.arange(span)
        return band_attend(q_blk, k_blk, v_blk, q_pos, k_pos, bias_table, sink)

    out = lax.map(one, (jnp.arange(nc), q_blocks))
    return jnp.moveaxis(out, 0, 1).reshape(q.shape)


def band_sample(q, k_new, v_new, k_cache, v_cache, past_len, bias_table, sink):
    lc, t = k_cache.shape[1], q.shape[1]
    k_all = jnp.concatenate([k_cache, k_new], axis=1)
    v_all = jnp.concatenate([v_cache, v_new], axis=1)
    q_pos = past_len + jnp.arange(t)
    k_pos = past_len - lc + jnp.arange(lc + t)
    out = band_attend(q, k_all, v_all, q_pos, k_pos, bias_table, sink)
    return out, k_all[:, t:], v_all[:, t:]


def sb_attend(q, k, v, q_pos, k_pos):
    bn, tq, h, d = q.shape
    tk = k.shape[1]
    nk = tk // SB_BLOCK
    z = jnp.einsum('bqhd,bkhd->bhqk', q.astype(jnp.float32), k.astype(jnp.float32)) * (d ** -0.5)
    mask = k_pos[None, :] < q_pos[:, None]
    ls = jnp.where(mask, jax.nn.log_sigmoid(-z), 0.0)
    lsb = ls.reshape(bn, h, tq, nk, SB_BLOCK)
    tri = jnp.tril(jnp.ones((SB_BLOCK, SB_BLOCK), jnp.float32), -1)
    tri_blk = jnp.tril(jnp.ones((nk, nk), jnp.float32), -1)
    local = jnp.einsum('bhqnj,js->bhqns', lsb, tri)
    after_blk = jnp.einsum('bhqm,mn->bhqn', jnp.sum(lsb, axis=-1), tri_blk)
    log_after = (local + after_blk[..., None]).reshape(bn, h, tq, tk)
    a = jnp.where(mask, jnp.exp(z + ls + log_after), 0.0)
    return jnp.einsum('bhqk,bkhd->bqhd', a, v.astype(jnp.float32)).astype(v.dtype)


def sb_prompt(q, k, v):
    s = q.shape[1]
    nb = s // SB_BLOCK
    outs = []
    for i in range(nb):
        lo, hi = i * SB_BLOCK, (i + 1) * SB_BLOCK
        outs.append(sb_attend(q[:, lo:hi], k[:, :hi], v[:, :hi], jnp.arange(lo, hi), jnp.arange(hi)))
    return jnp.concatenate(outs, axis=1)


def sb_sample(q, k_all, v_all, q_pos):
    tk = k_all.shape[1]
    pad = (-tk) % SB_BLOCK
    k_p = jnp.pad(k_all, ((0, 0), (0, pad), (0, 0), (0, 0)))
    v_p = jnp.pad(v_all, ((0, 0), (0, pad), (0, 0), (0, 0)))
    return sb_attend(q, k_p, v_p, q_pos, jnp.arange(tk + pad))


def ab_project(h, pos, w_in):
    bn, t, _ = h.shape
    qa, ka, va, qb, kb, vb = jnp.split(h @ w_in, AB_SPLITS, axis=-1)
    qa = rope(qa.reshape(bn, t, HA, HEAD_DIM), pos).reshape(bn, t, KV_A, G_A, HEAD_DIM)
    ka = rope(ka.reshape(bn, t, KV_A, HEAD_DIM), pos)
    va = va.reshape(bn, t, KV_A, HEAD_DIM)
    qb = qb.reshape(bn, t, HB, 1, HEAD_DIM)
    kb = kb.reshape(bn, t, HB, HEAD_DIM)
    vb = vb.reshape(bn, t, HB, HEAD_DIM)
    return qa, ka, va, qb, kb, vb


def merge_heads(oa, ob):
    bn, t = oa.shape[0], oa.shape[1]
    return jnp.concatenate([oa.reshape(bn, t, -1), ob.reshape(bn, t, -1)], axis=-1)


def c_project(h, w_in):
    bn, t, _ = h.shape
    q, k, v = jnp.split(h @ w_in, 3, axis=-1)
    shape = (bn, t, HC, HEAD_DIM)
    return q.reshape(shape), k.reshape(shape), v.reshape(shape)


def setup_inputs(seed: int = 0) -> dict:
    key = jax.random.key(seed)
    ks = jax.random.split(key, 20)

    def nrm(k, shape, scale):
        return jax.random.normal(k, shape, jnp.float32) * scale

    la = min(A_REACH, PAST_LEN)
    lb = min(B_REACH, PAST_LEN)
    return {
        'x_prompt': nrm(ks[0], (BATCH, SEQ, D_MODEL), 1.0),
        'x_sample': nrm(ks[1], (DEC_BATCH, DEC_SEQ, D_MODEL), 1.0),
        'cache_a_k': nrm(ks[2], (N_AB, DEC_BATCH, la, KV_A, HEAD_DIM), 1.0),
        'cache_a_v': nrm(ks[3], (N_AB, DEC_BATCH, la, KV_A, HEAD_DIM), 1.0),
        'cache_b_k': nrm(ks[4], (N_AB, DEC_BATCH, lb, HB, HEAD_DIM), 1.0),
        'cache_b_v': nrm(ks[5], (N_AB, DEC_BATCH, lb, HB, HEAD_DIM), 1.0),
        'cache_c_k': nrm(ks[6], (N_C, DEC_BATCH, PAST_LEN, HC, HEAD_DIM), 1.0),
        'cache_c_v': nrm(ks[7], (N_C, DEC_BATCH, PAST_LEN, HC, HEAD_DIM), 1.0),
        'norm_mix': 1.0 + nrm(ks[8], (DEPTH, D_MODEL), 0.02),
        'w_in_ab': nrm(ks[9], (N_AB, D_MODEL, AB_WIDTH), D_MODEL ** -0.5),
        'sink_a': nrm(ks[10], (N_AB, HA), 0.5),
        'rel_bias_b': nrm(ks[11], (N_AB, HB, N_REL), 0.1),
        'w_out_ab': nrm(ks[12], (N_AB, D_MODEL, D_MODEL), D_MODEL ** -0.5),
        'w_in_c': nrm(ks[13], (N_C, D_MODEL, 3 * HC * HEAD_DIM), D_MODEL ** -0.5),
        'w_out_c': nrm(ks[14], (N_C, HC * HEAD_DIM, D_MODEL), (HC * HEAD_DIM) ** -0.5),
        'norm_ffn': 1.0 + nrm(ks[15], (DEPTH, D_MODEL), 0.02),
        'w_gate': nrm(ks[16], (DEPTH, D_MODEL, D_FF), D_MODEL ** -0.5),
        'w_up': nrm(ks[17], (DEPTH, D_MODEL, D_FF), D_MODEL ** -0.5),
        'w_down': nrm(ks[18], (DEPTH, D_FF, D_MODEL), D_FF ** -0.5),
        'norm_final': 1.0 + nrm(ks[19], (D_MODEL,), 0.02),
    }


def reference(x_prompt, x_sample, cache_a_k, cache_a_v, cache_b_k, cache_b_v, cache_c_k, cache_c_v,
              norm_mix, w_in_ab, sink_a, rel_bias_b, w_out_ab, w_in_c, w_out_c, norm_ffn, w_gate, w_up,
              w_down, norm_final):
    seq = x_prompt.shape[1]
    dec_seq = x_sample.shape[1]
    past_len = cache_c_k.shape[2]
    pos_p = jnp.arange(seq)
    pos_s = past_len + jnp.arange(dec_seq)
    keep_a = min(A_REACH, seq)
    keep_b = min(B_REACH, seq)

    yp, ys = x_prompt, x_sample
    pa_k, pa_v, pb_k, pb_v, pc_k, pc_v = [], [], [], [], [], []
    sa_k, sa_v, sb_k, sb_v, sc_k, sc_v = [], [], [], [], [], []
    for layer in range(DEPTH):
        i = layer // 2
        hp = rmsnorm(yp, norm_mix[layer])
        hs = rmsnorm(ys, norm_mix[layer])
        if layer % 2 == 0:
            qa, ka, va, qb, kb, vb = ab_project(hp, pos_p, w_in_ab[i])
            oa = band_prompt(qa, ka, va, A_BAND_CHUNKS, None, sink_a[i])
            ob = band_prompt(qb, kb, vb, B_BAND_CHUNKS, rel_bias_b[i], None)
            mp = merge_heads(oa, ob) @ w_out_ab[i]
            pa_k.append(ka[:, seq - keep_a:])
            pa_v.append(va[:, seq - keep_a:])
            pb_k.append(kb[:, seq - keep_b:])
            pb_v.append(vb[:, seq - keep_b:])
            qa, ka, va, qb, kb, vb = ab_project(hs, pos_s, w_in_ab[i])
            oa, nak, nav = band_sample(qa, ka, va, cache_a_k[i], cache_a_v[i], past_len, None, sink_a[i])
            ob, nbk, nbv = band_sample(qb, kb, vb, cache_b_k[i], cache_b_v[i], past_len, rel_bias_b[i], None)
            ms = merge_heads(oa, ob) @ w_out_ab[i]
            sa_k.append(nak)
            sa_v.append(nav)
            sb_k.append(nbk)
            sb_v.append(nbv)
        else:
            qp, kp, vp = c_project(hp, w_in_c[i])
            mp = sb_prompt(qp, kp, vp).reshape(hp.shape[0], seq, HC * HEAD_DIM) @ w_out_c[i]
            pc_k.append(kp)
            pc_v.append(vp)
            qs, kss, vss = c_project(hs, w_in_c[i])
            k_all = jnp.concatenate([cache_c_k[i], kss], axis=1)
            v_all = jnp.concatenate([cache_c_v[i], vss], axis=1)
            ms = sb_sample(qs, k_all, v_all, pos_s).reshape(hs.shape[0], dec_seq, HC * HEAD_DIM) @ w_out_c[i]
            sc_k.append(kss)
            sc_v.append(vss)
        yp = yp + mp
        ys = ys + ms
        yp = yp + swiglu(rmsnorm(yp, norm_ffn[layer]), w_gate[layer], w_up[layer], w_down[layer])
        ys = ys + swiglu(rmsnorm(ys, norm_ffn[layer]), w_gate[layer], w_up[layer], w_down[layer])

    y_prompt = rmsnorm(yp, norm_final)
    y_sample = rmsnorm(ys, norm_final)
    new_a_k_prompt = jnp.stack(pa_k)
    new_a_v_prompt = jnp.stack(pa_v)
    new_b_k_prompt = jnp.stack(pb_k)
    new_b_v_prompt = jnp.stack(pb_v)
    new_c_k_prompt = jnp.stack(pc_k)
    new_c_v_prompt = jnp.stack(pc_v)
    new_a_k_sample = jnp.stack(sa_k)
    new_a_v_sample = jnp.stack(sa_v)
    new_b_k_sample = jnp.stack(sb_k)
    new_b_v_sample = jnp.stack(sb_v)
    new_c_k_sample = jnp.stack(sc_k)
    new_c_v_sample = jnp.stack(sc_v)
    return (y_prompt, y_sample, new_a_k_prompt, new_a_v_prompt, new_b_k_prompt, new_b_v_prompt,
            new_c_k_prompt, new_c_v_prompt, new_a_k_sample, new_a_v_sample, new_b_k_sample,
            new_b_v_sample, new_c_k_sample, new_c_v_sample)
```

```python
import functools

import jax
import jax.numpy as jnp
from jax import lax
from jax.experimental import pallas as pl
from jax.experimental.pallas import tpu as pltpu

HEAD_DIM = 128
CHUNK = 64
HA = 16
KV_A = 4
G_A = HA // KV_A
HB = 16
HC = 16
A_BAND_CHUNKS = 3
B_BAND_CHUNKS = 9
A_REACH = (A_BAND_CHUNKS - 1) * CHUNK
B_REACH = (B_BAND_CHUNKS - 1) * CHUNK
REL_CLIP = 256
N_REL = 2 * REL_CLIP + 1
SB_BLOCK = 128
ROPE_THETA = 10000.0
RMS_EPS = 1e-6
NEG_INF = -1e30
QA_W = HA * HEAD_DIM
KA_W = KV_A * HEAD_DIM
QB_W = HB * HEAD_DIM
AB_SPLITS = (QA_W, QA_W + KA_W, QA_W + 2 * KA_W, QA_W + 2 * KA_W + QB_W, QA_W + 2 * KA_W + 2 * QB_W)
QK_SCALE = HEAD_DIM ** -0.5

SUB = 128
BAND_QB = 512
VMEM_LIMIT_MB = 56

F32 = jnp.float32
BF16 = jnp.bfloat16


def _params(semantics):
    return pltpu.CompilerParams(dimension_semantics=semantics, vmem_limit_bytes=VMEM_LIMIT_MB << 20)


def _dot(a, b):
    return jnp.dot(a, b, preferred_element_type=F32)


def _dot_t(a, b):
    return lax.dot_general(a, b, (((1,), (1,)), ((), ())), preferred_element_type=F32)


def _rmsnorm_kernel(x_ref, g_ref, o_ref):
    x = x_ref[...]
    ms = jnp.mean(x * x, axis=-1, keepdims=True)
    o_ref[...] = (x * lax.rsqrt(ms + RMS_EPS) * g_ref[...]).astype(o_ref.dtype)


def rmsnorm(x, g, out_dtype):
    m, d = x.shape
    tm = min(256, m)
    return pl.pallas_call(
        _rmsnorm_kernel,
        grid=(m // tm,),
        in_specs=[pl.BlockSpec((tm, d), lambda i: (i, 0)), pl.BlockSpec((1, d), lambda i: (0, 0))],
        out_specs=pl.BlockSpec((tm, d), lambda i: (i, 0)),
        out_shape=jax.ShapeDtypeStruct((m, d), out_dtype),
        compiler_params=_params(("parallel",)),
        name="rmsnorm",
    )(x, g.reshape(1, d))


def _mm_kernel(a_ref, w_ref, o_ref, *, scale):
    acc = _dot(a_ref[...], w_ref[...])
    if scale is not None:
        acc = acc * scale
    o_ref[...] = acc.astype(o_ref.dtype)


def _mm_dual_kernel(a_ref, w_ref, o32_ref, o16_ref):
    acc = _dot(a_ref[...], w_ref[...])
    o32_ref[...] = acc
    o16_ref[...] = acc.astype(o16_ref.dtype)


def _mm_res_kernel(a_ref, w_ref, r_ref, o_ref):
    o_ref[...] = r_ref[...] + _dot(a_ref[...], w_ref[...])


def _mm2_res_kernel(a1_ref, a2_ref, w_ref, r_ref, o_ref):
    k1 = a1_ref.shape[1]
    acc = _dot(a1_ref[...], w_ref[:k1, :]) + _dot(a2_ref[...], w_ref[k1:, :])
    o_ref[...] = r_ref[...] + acc


def _gateup_kernel(a_ref, wg_ref, wu_ref, o_ref):
    a = a_ref[...]
    g = _dot(a, wg_ref[...])
    u = _dot(a, wu_ref[...])
    o_ref[...] = (g / (1.0 + jnp.exp(-g)) * u).astype(o_ref.dtype)


def _mm_ab_kernel(a_ref, w_ref, cos_ref, sin_ref, o_ref, *, tn):
    col0 = pl.program_id(1) * tn
    acc = _dot(a_ref[...], w_ref[...])
    is_rope = col0 < AB_SPLITS[1]
    is_q = (col0 < AB_SPLITS[0]) | ((col0 >= AB_SPLITS[2]) & (col0 < AB_SPLITS[3]))
    s = jnp.where(is_q, QK_SCALE, 1.0).astype(F32)

    @pl.when(is_rope)
    def _():
        cos = cos_ref[...]
        sin = sin_ref[...]
        for g in range(tn // HEAD_DIM):
            blk = acc[:, g * HEAD_DIM:(g + 1) * HEAD_DIM]
            rot = blk * cos + pltpu.roll(blk, HEAD_DIM // 2, axis=1) * sin
            o_ref[:, g * HEAD_DIM:(g + 1) * HEAD_DIM] = (rot * s).astype(o_ref.dtype)

    @pl.when(jnp.logical_not(is_rope))
    def _():
        o_ref[...] = (acc * s).astype(o_ref.dtype)


def _tiles(m, n, tm, tn):
    tm = min(tm, m)
    tn = min(tn, n)
    assert m % tm == 0 and n % tn == 0, (m, n, tm, tn)
    return tm, tn


def matmul(a, w, out_dtype, *, scale=None, tm=1024, tn=512):
    m, k = a.shape
    n = w.shape[1]
    tm, tn = _tiles(m, n, tm, tn)
    return pl.pallas_call(
        functools.partial(_mm_kernel, scale=scale),
        grid=(m // tm, n // tn),
        in_specs=[pl.BlockSpec((tm, k), lambda i, j: (i, 0)), pl.BlockSpec((k, tn), lambda i, j: (0, j))],
        out_specs=pl.BlockSpec((tm, tn), lambda i, j: (i, j)),
        out_shape=jax.ShapeDtypeStruct((m, n), out_dtype),
        compiler_params=_params(("parallel", "arbitrary")),
        name="matmul",
    )(a, w)


def matmul_dual(a, w, *, tm=1024, tn=512):
    m, k = a.shape
    n = w.shape[1]
    tm, tn = _tiles(m, n, tm, tn)
    return pl.pallas_call(
        _mm_dual_kernel,
        grid=(m // tm, n // tn),
        in_specs=[pl.BlockSpec((tm, k), lambda i, j: (i, 0)), pl.BlockSpec((k, tn), lambda i, j: (0, j))],
        out_specs=[pl.BlockSpec((tm, tn), lambda i, j: (i, j)), pl.BlockSpec((tm, tn), lambda i, j: (i, j))],
        out_shape=[jax.ShapeDtypeStruct((m, n), F32), jax.ShapeDtypeStruct((m, n), BF16)],
        compiler_params=_params(("parallel", "arbitrary")),
        name="matmul_dual",
    )(a, w)


def matmul_res(a, w, res, *, tm=512, tn=256):
    m, k = a.shape
    n = w.shape[1]
    tm, tn = _tiles(m, n, tm, tn)
    return pl.pallas_call(
        _mm_res_kernel,
        grid=(m // tm, n // tn),
        in_specs=[pl.BlockSpec((tm, k), lambda i, j: (i, 0)), pl.BlockSpec((k, tn), lambda i, j: (0, j)),
                  pl.BlockSpec((tm, tn), lambda i, j: (i, j))],
        out_specs=pl.BlockSpec((tm, tn), lambda i, j: (i, j)),
        out_shape=jax.ShapeDtypeStruct((m, n), F32),
        compiler_params=_params(("parallel", "arbitrary")),
        name="matmul_res",
    )(a, w, res)


def matmul2_res(a1, a2, w, res, *, tm=1024, tn=512):
    m, k1 = a1.shape
    k2 = a2.shape[1]
    n = w.shape[1]
    tm, tn = _tiles(m, n, tm, tn)
    return pl.pallas_call(
        _mm2_res_kernel,
        grid=(m // tm, n // tn),
        in_specs=[pl.BlockSpec((tm, k1), lambda i, j: (i, 0)), pl.BlockSpec((tm, k2), lambda i, j: (i, 0)),
                  pl.BlockSpec((k1 + k2, tn), lambda i, j: (0, j)), pl.BlockSpec((tm, tn), lambda i, j: (i, j))],
        out_specs=pl.BlockSpec((tm, tn), lambda i, j: (i, j)),
        out_shape=jax.ShapeDtypeStruct((m, n), F32),
        compiler_params=_params(("parallel", "arbitrary")),
        name="matmul2_res",
    )(a1, a2, w, res)


def gateup(a, wg, wu, *, tm=1024, tn=256):
    m, k = a.shape
    n = wg.shape[1]
    tm, tn = _tiles(m, n, tm, tn)
    return pl.pallas_call(
        _gateup_kernel,
        grid=(m // tm, n // tn),
        in_specs=[pl.BlockSpec((tm, k), lambda i, j: (i, 0)), pl.BlockSpec((k, tn), lambda i, j: (0, j)),
                  pl.BlockSpec((k, tn), lambda i, j: (0, j))],
        out_specs=pl.BlockSpec((tm, tn), lambda i, j: (i, j)),
        out_shape=jax.ShapeDtypeStruct((m, n), BF16),
        compiler_params=_params(("parallel", "arbitrary")),
        name="gateup",
    )(a, wg, wu)


def matmul_ab(a, w, cos, sin, out_dtype, *, tm=1024):
    m, k = a.shape
    n = w.shape[1]
    tn = 512
    assert all(s % tn == 0 for s in AB_SPLITS)
    tm, tn = _tiles(m, n, tm, tn)
    return pl.pallas_call(
        functools.partial(_mm_ab_kernel, tn=tn),
        grid=(m // tm, n // tn),
        in_specs=[pl.BlockSpec((tm, k), lambda i, j: (i, 0)), pl.BlockSpec((k, tn), lambda i, j: (0, j)),
                  pl.BlockSpec((tm, HEAD_DIM), lambda i, j: (i, 0)), pl.BlockSpec((tm, HEAD_DIM), lambda i, j: (i, 0))],
        out_specs=pl.BlockSpec((tm, tn), lambda i, j: (i, j)),
        out_shape=jax.ShapeDtypeStruct((m, n), out_dtype),
        compiler_params=_params(("parallel", "arbitrary")),
        name="matmul_ab",
    )(a, w, cos, sin)


def _band_mask(rows, halo, n_band):
    r = lax.broadcasted_iota(jnp.int32, (rows, halo + SUB), 0)
    c = lax.broadcasted_iota(jnp.int32, (rows, halo + SUB), 1)
    rc = (r + halo) >> 6
    cc = c >> 6
    return (cc <= rc) & (cc > rc - n_band)


def _bias_prep_kernel(tab_ref, o_ref):
    h = pl.program_id(0)
    width = B_REACH + SUB
    ulen = width + SUB
    n = lax.broadcasted_iota(jnp.int32, (8, ulen), 1)
    idx = jnp.clip(B_REACH + SUB - 1 - n, -REL_CLIP, REL_CLIP) + REL_CLIP

    def body(m, u):
        return jnp.where(idx == m, tab_ref[h, m], u)

    u = lax.fori_loop(0, N_REL, body, jnp.zeros((8, ulen), F32))
    x = jnp.broadcast_to(u[0:1, :], (SUB, ulen))
    x = pltpu.roll(x, ulen - (SUB - 1), axis=1, stride=1, stride_axis=0)
    o_ref[0] = jnp.where(_band_mask(SUB, B_REACH, B_BAND_CHUNKS), x[:, :width], NEG_INF)


def bias_prep(table):
    return pl.pallas_call(
        _bias_prep_kernel,
        grid=(HB,),
        in_specs=[pl.BlockSpec(memory_space=pltpu.SMEM)],
        out_specs=pl.BlockSpec((1, SUB, B_REACH + SUB), lambda h: (h, 0, 0)),
        out_shape=jax.ShapeDtypeStruct((HB, SUB, B_REACH + SUB), F32),
        compiler_params=_params(("arbitrary",)),
        name="bias_prep",
    )(table)


def _band_prompt_kernel(*refs, group, halo, n_band, has_bias, has_sink):
    q_ref, km_ref, kh_ref, vm_ref, vh_ref = refs[:5]
    rest = list(refs[5:])
    bias_ref = rest.pop(0) if has_bias else None
    sink_ref = rest.pop(0) if has_sink else None
    o_ref = rest.pop(0)
    h = pl.program_id(0)
    i = pl.program_id(1)
    qb = q_ref.shape[0]
    width = halo + SUB
    k = jnp.concatenate([kh_ref[...], km_ref[...]], axis=0)
    v = jnp.concatenate([vh_ref[...], vm_ref[...]], axis=0)
    c = lax.broadcasted_iota(jnp.int32, (SUB, width), 1)
    if has_bias:
        bias = bias_ref[0]
    else:
        bias = jnp.where(_band_mask(SUB, halo, n_band), 0.0, NEG_INF).astype(F32)
    for sb in range(qb // SUB):
        kw = k[sb * SUB:sb * SUB + width]
        vw = v[sb * SUB:sb * SUB + width]
        first_valid = halo - i * qb - sb * SUB
        for g in range(group):
            q = q_ref[sb * SUB:(sb + 1) * SUB, g * HEAD_DIM:(g + 1) * HEAD_DIM]
            s = _dot_t(q, kw) + bias
            s = jnp.where(c >= first_valid, s, NEG_INF)
            m = jnp.max(s, axis=-1, keepdims=True)
            if has_sink:
                sk = sink_ref[h * group + g]
                m = jnp.maximum(m, sk)
            p = jnp.exp(s - m)
            l = jnp.sum(p, axis=-1, keepdims=True)
            if has_sink:
                l = l + jnp.exp(sk - m)
            o = _dot(p.astype(BF16), vw) / l
            o_ref[sb * SUB:(sb + 1) * SUB, g * HEAD_DIM:(g + 1) * HEAD_DIM] = o.astype(o_ref.dtype)


def band_prompt(proj, *, q_col, k_col, v_col, n_kv, group, halo, n_band, bias=None, sink=None):
    s = proj.shape[0]
    qb = BAND_QB
    assert s % qb == 0 and qb % halo == 0 and halo % CHUNK == 0
    qw = group * HEAD_DIM
    hpb = qb // halo
    qc, kc, vc = q_col // qw, k_col // HEAD_DIM, v_col // HEAD_DIM
    main = lambda c0: pl.BlockSpec((qb, HEAD_DIM), lambda h, i: (i, c0 + h))
    halo_spec = lambda c0: pl.BlockSpec((halo, HEAD_DIM), lambda h, i: (jnp.maximum(i * hpb - 1, 0), c0 + h))
    in_specs = [pl.BlockSpec((qb, qw), lambda h, i: (i, qc + h)), main(kc), halo_spec(kc), main(vc), halo_spec(vc)]
    args = [proj, proj, proj, proj, proj]
    if bias is not None:
        in_specs.append(pl.BlockSpec((1, SUB, halo + SUB), lambda h, i: (h, 0, 0)))
        args.append(bias)
    if sink is not None:
        in_specs.append(pl.BlockSpec(memory_space=pltpu.SMEM))
        args.append(sink)
    return pl.pallas_call(
        functools.partial(_band_prompt_kernel, group=group, halo=halo, n_band=n_band,
                          has_bias=bias is not None, has_sink=sink is not None),
        grid=(n_kv, s // qb),
        in_specs=in_specs,
        out_specs=pl.BlockSpec((qb, qw), lambda h, i: (i, h)),
        out_shape=jax.ShapeDtypeStruct((s, n_kv * qw), BF16),
        compiler_params=_params(("parallel", "arbitrary")),
        name="band_prompt",
    )(*args)


def _band_sample_kernel(*refs, group, has_bias, has_sink):
    q_ref, kn_ref, vn_ref, kc_ref, vc_ref = refs[:5]
    rest = list(refs[5:])
    bias_ref = rest.pop(0) if has_bias else None
    sink_ref = rest.pop(0) if has_sink else None
    o_ref, ko_ref, vo_ref = rest
    h = pl.program_id(1)
    t = q_ref.shape[0]
    lc = kc_ref.shape[0]
    kc = kc_ref[...]
    vc = vc_ref[...]
    kn = kn_ref[...]
    vn = vn_ref[...]
    ko_ref[:lc - t, :] = kc[t:, :]
    ko_ref[lc - t:, :] = kn
    vo_ref[:lc - t, :] = vc[t:, :]
    vo_ref[lc - t:, :] = vn
    q = jnp.concatenate([q_ref[:, g * HEAD_DIM:(g + 1) * HEAD_DIM] for g in range(group)], axis=0).astype(BF16)
    sc = _dot_t(q, kc.astype(BF16))
    sn = _dot_t(q, kn.astype(BF16))
    if has_bias:
        sc = sc + bias_ref[0, :t, :lc]
        sn = sn + bias_ref[0, :t, lc:lc + t]
    m = jnp.maximum(jnp.max(sc, axis=-1, keepdims=True), jnp.max(sn, axis=-1, keepdims=True))
    if has_sink:
        row = lax.broadcasted_iota(jnp.int32, (group * t, 1), 0)
        sk = jnp.zeros((group * t, 1), F32)
        for g in range(group):
            sk = jnp.where((row >= g * t) & (row < (g + 1) * t), sink_ref[h * group + g], sk)
        m = jnp.maximum(m, sk)
    pc = jnp.exp(sc - m)
    pn = jnp.exp(sn - m)
    l = jnp.sum(pc, axis=-1, keepdims=True) + jnp.sum(pn, axis=-1, keepdims=True)
    if has_sink:
        l = l + jnp.exp(sk - m)
    o = (_dot(pc.astype(BF16), vc.astype(BF16)) + _dot(pn.astype(BF16), vn.astype(BF16))) / l
    for g in range(group):
        o_ref[:, g * HEAD_DIM:(g + 1) * HEAD_DIM] = o[g * t:(g + 1) * t].astype(o_ref.dtype)


def band_sample(proj, cache_k, cache_v, *, t, q_col, k_col, v_col, n_kv, group, bias=None, sink=None):
    nb, lc, _ = cache_k.shape
    qw = group * HEAD_DIM
    qc, kc, vc = q_col // qw, k_col // HEAD_DIM, v_col // HEAD_DIM
    new = lambda c0: pl.BlockSpec((t, HEAD_DIM), lambda b, h: (b, c0 + h))
    cache_spec = pl.BlockSpec((None, lc, HEAD_DIM), lambda b, h: (b, 0, h))
    in_specs = [pl.BlockSpec((t, qw), lambda b, h: (b, qc + h)), new(kc), new(vc), cache_spec, cache_spec]
    args = [proj, proj, proj, cache_k, cache_v]
    if bias is not None:
        assert lc == B_REACH and t <= CHUNK
        in_specs.append(pl.BlockSpec((1, SUB, B_REACH + SUB), lambda b, h: (h, 0, 0)))
        args.append(bias)
    if sink is not None:
        in_specs.append(pl.BlockSpec(memory_space=pltpu.SMEM))
        args.append(sink)
    return pl.pallas_call(
        functools.partial(_band_sample_kernel, group=group, has_bias=bias is not None, has_sink=sink is not None),
        grid=(nb, n_kv),
        in_specs=in_specs,
        out_specs=[pl.BlockSpec((t, qw), lambda b, h: (b, h)), cache_spec, cache_spec],
        out_shape=[jax.ShapeDtypeStruct((nb * t, n_kv * qw), BF16),
                   jax.ShapeDtypeStruct(cache_k.shape, F32), jax.ShapeDtypeStruct(cache_v.shape, F32)],
        compiler_params=_params(("parallel", "arbitrary")),
        name="band_sample",
    )(*args)


def _tri2():
    j = lax.broadcasted_iota(jnp.int32, (SB_BLOCK, 2 * SB_BLOCK), 0)
    s = lax.broadcasted_iota(jnp.int32, (SB_BLOCK, 2 * SB_BLOCK), 1)
    return jnp.where((s >= SB_BLOCK) | (j > s), 1.0, 0.0).astype(BF16)


def _neg_log_one_minus_beta(z):
    return jnp.maximum(z, 0.0) + jnp.log1p(jnp.exp(-jnp.abs(z)))


def _sb_diag(q, kb, vb, tri2):
    rows = q.shape[0]
    r = lax.broadcasted_iota(jnp.int32, (rows, SB_BLOCK), 0)
    c = lax.broadcasted_iota(jnp.int32, (rows, SB_BLOCK), 1)
    mask = c < r
    z = _dot_t(q, kb)
    nls = jnp.where(mask, _neg_log_one_minus_beta(z), 0.0)
    la = _dot(nls.astype(BF16), tri2)
    a = jnp.where(mask, jnp.exp(z - nls - la[:, :SB_BLOCK]), 0.0)
    return _dot(a.astype(BF16), vb), la[:, SB_BLOCK:]


def _sb_full(q, kb, vb, tri2, acc, carry):
    z = _dot_t(q, kb)
    nls = _neg_log_one_minus_beta(z)
    la = _dot(nls.astype(BF16), tri2)
    a = jnp.exp(z - nls - la[:, :SB_BLOCK] - carry)
    return acc + _dot(a.astype(BF16), vb), carry + la[:, SB_BLOCK:]


def _sb_prompt_kernel(q_ref, k_ref, v_ref, o_ref):
    i = pl.program_id(1)
    q = q_ref[...]
    tri2 = _tri2()

    def blk(ref, j):
        return ref[pl.ds(pl.multiple_of(j * SB_BLOCK, SB_BLOCK), SB_BLOCK), :]

    acc, carry = _sb_diag(q, blk(k_ref, i), blk(v_ref, i), tri2)

    def body(step, state):
        j = i - 1 - step
        return _sb_full(q, blk(k_ref, j), blk(v_ref, j), tri2, *state)

    acc, _ = lax.fori_loop(0, i, body, (acc, carry))
    o_ref[...] = acc.astype(o_ref.dtype)


def sb_prompt(q, k, v):
    s = q.shape[0]
    assert s % SB_BLOCK == 0
    head_all = pl.BlockSpec((s, HEAD_DIM), lambda h, i: (0, h))
    return pl.pallas_call(
        _sb_prompt_kernel,
        grid=(HC, s // SB_BLOCK),
        in_specs=[pl.BlockSpec((SB_BLOCK, HEAD_DIM), lambda h, i: (i, h)), head_all, head_all],
        out_specs=pl.BlockSpec((SB_BLOCK, HEAD_DIM), lambda h, i: (i, h)),
        out_shape=jax.ShapeDtypeStruct(q.shape, BF16),
        compiler_params=_params(("parallel", "arbitrary")),
        name="sb_prompt",
    )(q, k, v)


def _sb_sample_kernel(q_ref, kn_ref, vn_ref, kc_ref, vc_ref, o_ref):
    t = q_ref.shape[0]
    nblk = kc_ref.shape[0] // SB_BLOCK
    q = q_ref[...].astype(BF16)
    tri2 = _tri2()
    zeros = jnp.zeros((SB_BLOCK - t, HEAD_DIM), BF16)
    kb = jnp.concatenate([kn_ref[...].astype(BF16), zeros], axis=0)
    vb = jnp.concatenate([vn_ref[...].astype(BF16), zeros], axis=0)
    acc, carry = _sb_diag(q, kb, vb, tri2)

    def blk(ref, j):
        return ref[pl.ds(pl.multiple_of(j * SB_BLOCK, SB_BLOCK), SB_BLOCK), :].astype(BF16)

    def body(step, state):
        j = nblk - 1 - step
        return _sb_full(q, blk(kc_ref, j), blk(vc_ref, j), tri2, *state)

    acc, _ = lax.fori_loop(0, nblk, body, (acc, carry))
    o_ref[...] = acc.astype(o_ref.dtype)


def sb_sample(q, k_new, v_new, cache_k, cache_v, *, t):
    nb, past, _ = cache_k.shape
    assert past % SB_BLOCK == 0 and t <= SB_BLOCK
    new = pl.BlockSpec((t, HEAD_DIM), lambda b, h: (b, h))
    cache_spec = pl.BlockSpec((None, past, HEAD_DIM), lambda b, h: (b, 0, h))
    return pl.pallas_call(
        _sb_sample_kernel,
        grid=(nb, HC),
        in_specs=[new, new, new, cache_spec, cache_spec],
        out_specs=new,
        out_shape=jax.ShapeDtypeStruct(q.shape, BF16),
        compiler_params=_params(("parallel", "arbitrary")),
        name="sb_sample",
    )(q, k_new, v_new, cache_k, cache_v)


def _rope_tables(pos):
    half = HEAD_DIM // 2
    inv = ROPE_THETA ** (-jnp.arange(half, dtype=F32) / half)
    ang = pos.astype(F32)[:, None] * inv[None, :]
    cos, sin = jnp.cos(ang), jnp.sin(ang)
    return jnp.concatenate([cos, cos], axis=-1), jnp.concatenate([-sin, sin], axis=-1)


def kernel(x_prompt, x_sample, cache_a_k, cache_a_v, cache_b_k, cache_b_v, cache_c_k, cache_c_v, norm_mix, w_in_ab, sink_a, rel_bias_b, w_out_ab, w_in_c, w_out_c, norm_ffn, w_gate, w_up, w_down, norm_final):
    bp, seq, d = x_prompt.shape
    nb, t, _ = x_sample.shape
    depth = norm_mix.shape[0]
    past = cache_c_k.shape[2]
    assert bp == 1 and seq >= B_REACH
    keep_a, keep_b = min(A_REACH, seq), min(B_REACH, seq)
    wc = HC * HEAD_DIM

    yp = x_prompt.reshape(seq, d)
    ys = x_sample.reshape(nb * t, d)
    cos_p, sin_p = _rope_tables(jnp.arange(seq))
    cos_s, sin_s = _rope_tables(past + jnp.arange(t))
    cos_s, sin_s = jnp.tile(cos_s, (nb, 1)), jnp.tile(sin_s, (nb, 1))

    outs = {name: [] for name in ("pa_k", "pa_v", "pb_k", "pb_v", "pc_k", "pc_v",
                                  "sa_k", "sa_v", "sb_k", "sb_v", "sc_k", "sc_v")}
    for layer in range(depth):
        i = layer // 2
        hp = rmsnorm(yp, norm_mix[layer], BF16)
        hs = rmsnorm(ys, norm_mix[layer], BF16)
        if layer % 2 == 0:
            w_in = w_in_ab[i].astype(BF16)
            w_out = w_out_ab[i].astype(BF16)
            bias = bias_prep(rel_bias_b[i])
            proj = matmul_ab(hp, w_in, cos_p, sin_p, BF16)
            oa = band_prompt(proj, q_col=0, k_col=AB_SPLITS[0], v_col=AB_SPLITS[1], n_kv=KV_A, group=G_A,
                             halo=A_REACH, n_band=A_BAND_CHUNKS, sink=sink_a[i])
            ob = band_prompt(proj, q_col=AB_SPLITS[2], k_col=AB_SPLITS[3], v_col=AB_SPLITS[4], n_kv=HB, group=1,
                             halo=B_REACH, n_band=B_BAND_CHUNKS, bias=bias)
            yp = matmul2_res(oa, ob, w_out, yp)
            tail = matmul_ab(hp[seq - keep_b:], w_in, cos_p[seq - keep_b:], sin_p[seq - keep_b:], F32)
            outs["pa_k"].append(tail[keep_b - keep_a:, AB_SPLITS[0]:AB_SPLITS[1]].reshape(1, keep_a, KV_A, HEAD_DIM))
            outs["pa_v"].append(tail[keep_b - keep_a:, AB_SPLITS[1]:AB_SPLITS[2]].reshape(1, keep_a, KV_A, HEAD_DIM))
            outs["pb_k"].append(tail[:, AB_SPLITS[3]:AB_SPLITS[4]].reshape(1, keep_b, HB, HEAD_DIM))
            outs["pb_v"].append(tail[:, AB_SPLITS[4]:].reshape(1, keep_b, HB, HEAD_DIM))
            projs = matmul_ab(hs, w_in, cos_s, sin_s, F32)
            la, lb = cache_a_k.shape[2], cache_b_k.shape[2]
            oa, nak, nav = band_sample(projs, cache_a_k[i].reshape(nb, la, KA_W), cache_a_v[i].reshape(nb, la, KA_W),
                                       t=t, q_col=0, k_col=AB_SPLITS[0], v_col=AB_SPLITS[1], n_kv=KV_A, group=G_A,
                                       sink=sink_a[i])
            ob, nbk, nbv = band_sample(projs, cache_b_k[i].reshape(nb, lb, QB_W), cache_b_v[i].reshape(nb, lb, QB_W),
                                       t=t, q_col=AB_SPLITS[2], k_col=AB_SPLITS[3], v_col=AB_SPLITS[4], n_kv=HB,
                                       group=1, bias=bias)
            ys = matmul2_res(oa, ob, w_out, ys)
            outs["sa_k"].append(nak.reshape(nb, la, KV_A, HEAD_DIM))
            outs["sa_v"].append(nav.reshape(nb, la, KV_A, HEAD_DIM))
            outs["sb_k"].append(nbk.reshape(nb, lb, HB, HEAD_DIM))
            outs["sb_v"].append(nbv.reshape(nb, lb, HB, HEAD_DIM))
        else:
            w_in = w_in_c[i].astype(BF16)
            w_out = w_out_c[i].astype(BF16)
            wq, wk, wv = w_in[:, :wc], w_in[:, wc:2 * wc], w_in[:, 2 * wc:]
            q = matmul(hp, wq, BF16, scale=QK_SCALE)
            k32, k16 = matmul_dual(hp, wk)
            v32, v16 = matmul_dual(hp, wv)
            att = sb_prompt(q, k16, v16)
            yp = matmul_res(att, w_out, yp, tm=1024, tn=512)
            outs["pc_k"].append(k32.reshape(1, seq, HC, HEAD_DIM))
            outs["pc_v"].append(v32.reshape(1, seq, HC, HEAD_DIM))
            qs = matmul(hs, wq, F32, scale=QK_SCALE)
            ks = matmul(hs, wk, F32)
            vs = matmul(hs, wv, F32)
            att = sb_sample(qs, ks, vs, cache_c_k[i].reshape(nb, past, wc), cache_c_v[i].reshape(nb, past, wc), t=t)
            ys = matmul_res(att, w_out, ys, tm=1024, tn=512)
            outs["sc_k"].append(ks.reshape(nb, t, HC, HEAD_DIM))
            outs["sc_v"].append(vs.reshape(nb, t, HC, HEAD_DIM))
        wg, wu, wd = w_gate[layer].astype(BF16), w_up[layer].astype(BF16), w_down[layer].astype(BF16)
        yp = matmul_res(gateup(rmsnorm(yp, norm_ffn[layer], BF16), wg, wu), wd, yp)
        ys = matmul_res(gateup(rmsnorm(ys, norm_ffn[layer], BF16), wg, wu), wd, ys)

    y_prompt = rmsnorm(yp, norm_final, F32).reshape(1, seq, d)
    y_sample = rmsnorm(ys, norm_final, F32).reshape(nb, t, d)
    st = {name: jnp.stack(v) for name, v in outs.items()}
    return (y_prompt, y_sample, st["pa_k"], st["pa_v"], st["pb_k"], st["pb_v"], st["pc_k"], st["pc_v"],
            st["sa_k"], st["sa_v"], st["sb_k"], st["sb_v"], st["sc_k"], st["sc_v"])
```

```python
import functools

import jax
import jax.numpy as jnp
from jax import lax
from jax.experimental import pallas as pl
from jax.experimental.pallas import tpu as pltpu

HEAD_DIM = 128
CHUNK = 64
HA = 16
KV_A = 4
G_A = HA // KV_A
HB = 16
HC = 16
A_BAND_CHUNKS = 3
B_BAND_CHUNKS = 9
A_REACH = (A_BAND_CHUNKS - 1) * CHUNK
B_REACH = (B_BAND_CHUNKS - 1) * CHUNK
REL_CLIP = 256
N_REL = 2 * REL_CLIP + 1
SB_BLOCK = 128
ROPE_THETA = 10000.0
RMS_EPS = 1e-6
NEG_INF = -1e30
QA_W = HA * HEAD_DIM
KA_W = KV_A * HEAD_DIM
QB_W = HB * HEAD_DIM
AB_SPLITS = (QA_W, QA_W + KA_W, QA_W + 2 * KA_W, QA_W + 2 * KA_W + QB_W, QA_W + 2 * KA_W + 2 * QB_W)
QK_SCALE = HEAD_DIM ** -0.5

SB_QUERY_ROWS = 512
SB_SKIP_LOG = 106.0
SUB = 128
BAND_QB = 512
VMEM_LIMIT_MB = 56

F32 = jnp.float32
BF16 = jnp.bfloat16


def _params(semantics):
    return pltpu.CompilerParams(dimension_semantics=semantics, vmem_limit_bytes=VMEM_LIMIT_MB << 20)


def _dot(a, b):
    return jnp.dot(a, b, preferred_element_type=F32)


def _dot_t(a, b):
    return lax.dot_general(a, b, (((1,), (1,)), ((), ())), preferred_element_type=F32)


def _rmsnorm_kernel(x_ref, g_ref, o_ref):
    x = x_ref[...]
    ms = jnp.mean(x * x, axis=-1, keepdims=True)
    o_ref[...] = (x * lax.rsqrt(ms + RMS_EPS) * g_ref[...]).astype(o_ref.dtype)


def rmsnorm(x, g, out_dtype):
    m, d = x.shape
    tm = min(256, m)
    return pl.pallas_call(
        _rmsnorm_kernel,
        grid=(m // tm,),
        in_specs=[pl.BlockSpec((tm, d), lambda i: (i, 0)), pl.BlockSpec((1, d), lambda i: (0, 0))],
        out_specs=pl.BlockSpec((tm, d), lambda i: (i, 0)),
        out_shape=jax.ShapeDtypeStruct((m, d), out_dtype),
        compiler_params=_params(("parallel",)),
        name="rmsnorm",
    )(x, g.reshape(1, d))


def _cast_kernel(x_ref, o_ref):
    o_ref[...] = x_ref[...].astype(o_ref.dtype)


def cast_bf16(w_stack, layer):
    _, k, n = w_stack.shape
    tr = min(256, k)
    assert k % tr == 0
    return pl.pallas_call(
        _cast_kernel,
        grid=(k // tr,),
        in_specs=[pl.BlockSpec((None, tr, n), lambda i: (layer, i, 0))],
        out_specs=pl.BlockSpec((tr, n), lambda i: (i, 0)),
        out_shape=jax.ShapeDtypeStruct((k, n), BF16),
        compiler_params=_params(("parallel",)),
        name="cast_bf16",
    )(w_stack)


def _mm_kernel(a_ref, w_ref, o_ref, *, scale):
    acc = _dot(a_ref[...], w_ref[...])
    if scale is not None:
        acc = acc * scale
    o_ref[...] = acc.astype(o_ref.dtype)


def _mm_dual_kernel(a_ref, w_ref, o32_ref, o16_ref):
    acc = _dot(a_ref[...], w_ref[...])
    o32_ref[...] = acc
    o16_ref[...] = acc.astype(o16_ref.dtype)


def _mm_res_kernel(a_ref, w_ref, r_ref, o_ref):
    o_ref[...] = r_ref[...] + _dot(a_ref[...], w_ref[...])


def _mm2_res_kernel(a1_ref, a2_ref, w_ref, r_ref, o_ref):
    k1 = a1_ref.shape[1]
    acc = _dot(a1_ref[...], w_ref[:k1, :]) + _dot(a2_ref[...], w_ref[k1:, :])
    o_ref[...] = r_ref[...] + acc


def _gateup_kernel(a_ref, wg_ref, wu_ref, o_ref):
    a = a_ref[...]
    g = _dot(a, wg_ref[...])
    u = _dot(a, wu_ref[...])
    o_ref[...] = (g / (1.0 + jnp.exp(-g)) * u).astype(o_ref.dtype)


def _mm_ab_kernel(a_ref, w_ref, cos_ref, sin_ref, o_ref, *, tn):
    col0 = pl.program_id(1) * tn
    acc = _dot(a_ref[...], w_ref[...])
    is_rope = col0 < AB_SPLITS[1]
    is_q = (col0 < AB_SPLITS[0]) | ((col0 >= AB_SPLITS[2]) & (col0 < AB_SPLITS[3]))
    s = jnp.where(is_q, QK_SCALE, 1.0).astype(F32)

    @pl.when(is_rope)
    def _():
        cos = cos_ref[...]
        sin = sin_ref[...]
        for g in range(tn // HEAD_DIM):
            blk = acc[:, g * HEAD_DIM:(g + 1) * HEAD_DIM]
            rot = blk * cos + pltpu.roll(blk, HEAD_DIM // 2, axis=1) * sin
            o_ref[:, g * HEAD_DIM:(g + 1) * HEAD_DIM] = (rot * s).astype(o_ref.dtype)

    @pl.when(jnp.logical_not(is_rope))
    def _():
        o_ref[...] = (acc * s).astype(o_ref.dtype)


def _tiles(m, n, tm, tn):
    tm = min(tm, m)
    tn = min(tn, n)
    assert m % tm == 0 and n % tn == 0, (m, n, tm, tn)
    return tm, tn


def matmul(a, w, out_dtype, *, scale=None, col0=0, n=None, tm=1024, tn=512):
    m, k = a.shape
    n = w.shape[1] - col0 if n is None else n
    tm, tn = _tiles(m, n, tm, tn)
    assert col0 % tn == 0
    c0 = col0 // tn
    return pl.pallas_call(
        functools.partial(_mm_kernel, scale=scale),
        grid=(m // tm, n // tn),
        in_specs=[pl.BlockSpec((tm, k), lambda i, j: (i, 0)), pl.BlockSpec((k, tn), lambda i, j: (0, c0 + j))],
        out_specs=pl.BlockSpec((tm, tn), lambda i, j: (i, j)),
        out_shape=jax.ShapeDtypeStruct((m, n), out_dtype),
        compiler_params=_params(("parallel", "arbitrary")),
        name="matmul",
    )(a, w)


def matmul_dual(a, w, *, col0, n, tm=1024, tn=512):
    m, k = a.shape
    tm, tn = _tiles(m, n, tm, tn)
    assert col0 % tn == 0
    c0 = col0 // tn
    return pl.pallas_call(
        _mm_dual_kernel,
        grid=(m // tm, n // tn),
        in_specs=[pl.BlockSpec((tm, k), lambda i, j: (i, 0)), pl.BlockSpec((k, tn), lambda i, j: (0, c0 + j))],
        out_specs=[pl.BlockSpec((tm, tn), lambda i, j: (i, j)), pl.BlockSpec((tm, tn), lambda i, j: (i, j))],
        out_shape=[jax.ShapeDtypeStruct((m, n), F32), jax.ShapeDtypeStruct((m, n), BF16)],
        compiler_params=_params(("parallel", "arbitrary")),
        name="matmul_dual",
    )(a, w)


def matmul_res(a, w, res, *, tm=512, tn=256):
    m, k = a.shape
    n = w.shape[1]
    tm, tn = _tiles(m, n, tm, tn)
    return pl.pallas_call(
        _mm_res_kernel,
        grid=(m // tm, n // tn),
        in_specs=[pl.BlockSpec((tm, k), lambda i, j: (i, 0)), pl.BlockSpec((k, tn), lambda i, j: (0, j)),
                  pl.BlockSpec((tm, tn), lambda i, j: (i, j))],
        out_specs=pl.BlockSpec((tm, tn), lambda i, j: (i, j)),
        out_shape=jax.ShapeDtypeStruct((m, n), F32),
        compiler_params=_params(("parallel", "arbitrary")),
        name="matmul_res",
    )(a, w, res)


def matmul2_res(a1, a2, w, res, *, tm=1024, tn=512):
    m, k1 = a1.shape
    k2 = a2.shape[1]
    n = w.shape[1]
    tm, tn = _tiles(m, n, tm, tn)
    return pl.pallas_call(
        _mm2_res_kernel,
        grid=(m // tm, n // tn),
        in_specs=[pl.BlockSpec((tm, k1), lambda i, j: (i, 0)), pl.BlockSpec((tm, k2), lambda i, j: (i, 0)),
                  pl.BlockSpec((k1 + k2, tn), lambda i, j: (0, j)), pl.BlockSpec((tm, tn), lambda i, j: (i, j))],
        out_specs=pl.BlockSpec((tm, tn), lambda i, j: (i, j)),
        out_shape=jax.ShapeDtypeStruct((m, n), F32),
        compiler_params=_params(("parallel", "arbitrary")),
        name="matmul2_res",
    )(a1, a2, w, res)


def gateup(a, wg, wu, *, tm=1024, tn=256):
    m, k = a.shape
    n = wg.shape[1]
    tm, tn = _tiles(m, n, tm, tn)
    return pl.pallas_call(
        _gateup_kernel,
        grid=(m // tm, n // tn),
        in_specs=[pl.BlockSpec((tm, k), lambda i, j: (i, 0)), pl.BlockSpec((k, tn), lambda i, j: (0, j)),
                  pl.BlockSpec((k, tn), lambda i, j: (0, j))],
        out_specs=pl.BlockSpec((tm, tn), lambda i, j: (i, j)),
        out_shape=jax.ShapeDtypeStruct((m, n), BF16),
        compiler_params=_params(("parallel", "arbitrary")),
        name="gateup",
    )(a, wg, wu)


def matmul_ab(a, w, cos, sin, out_dtype, *, tm=1024):
    m, k = a.shape
    n = w.shape[1]
    tn = 512
    assert all(s % tn == 0 for s in AB_SPLITS)
    tm, tn = _tiles(m, n, tm, tn)
    return pl.pallas_call(
        functools.partial(_mm_ab_kernel, tn=tn),
        grid=(m // tm, n // tn),
        in_specs=[pl.BlockSpec((tm, k), lambda i, j: (i, 0)), pl.BlockSpec((k, tn), lambda i, j: (0, j)),
                  pl.BlockSpec((tm, HEAD_DIM), lambda i, j: (i, 0)), pl.BlockSpec((tm, HEAD_DIM), lambda i, j: (i, 0))],
        out_specs=pl.BlockSpec((tm, tn), lambda i, j: (i, j)),
        out_shape=jax.ShapeDtypeStruct((m, n), out_dtype),
        compiler_params=_params(("parallel", "arbitrary")),
        name="matmul_ab",
    )(a, w, cos, sin)


def _band_mask(rows, halo, n_band):
    r = lax.broadcasted_iota(jnp.int32, (rows, halo + SUB), 0)
    c = lax.broadcasted_iota(jnp.int32, (rows, halo + SUB), 1)
    rc = (r + halo) >> 6
    cc = c >> 6
    return (cc <= rc) & (cc > rc - n_band)


def _bias_prep_kernel(tab_ref, o_ref):
    h = pl.program_id(0)
    width = B_REACH + SUB
    ulen = width + SUB
    n = lax.broadcasted_iota(jnp.int32, (8, ulen), 1)
    idx = jnp.clip(B_REACH + SUB - 1 - n, -REL_CLIP, REL_CLIP) + REL_CLIP

    def body(m, u):
        return jnp.where(idx == m, tab_ref[h, m], u)

    u = lax.fori_loop(0, N_REL, body, jnp.zeros((8, ulen), F32))
    x = jnp.broadcast_to(u[0:1, :], (SUB, ulen))
    x = pltpu.roll(x, ulen - (SUB - 1), axis=1, stride=1, stride_axis=0)
    o_ref[0] = jnp.where(_band_mask(SUB, B_REACH, B_BAND_CHUNKS), x[:, :width], NEG_INF)


def bias_prep(table):
    return pl.pallas_call(
        _bias_prep_kernel,
        grid=(HB,),
        in_specs=[pl.BlockSpec(memory_space=pltpu.SMEM)],
        out_specs=pl.BlockSpec((1, SUB, B_REACH + SUB), lambda h: (h, 0, 0)),
        out_shape=jax.ShapeDtypeStruct((HB, SUB, B_REACH + SUB), F32),
        compiler_params=_params(("arbitrary",)),
        name="bias_prep",
    )(table)


def _band_prompt_kernel(*refs, group, halo, n_band, has_bias, has_sink):
    q_ref, km_ref, kh_ref, vm_ref, vh_ref = refs[:5]
    rest = list(refs[5:])
    bias_ref = rest.pop(0) if has_bias else None
    sink_ref = rest.pop(0) if has_sink else None
    o_ref = rest.pop(0)
    h = pl.program_id(0)
    i = pl.program_id(1)
    qb = q_ref.shape[0]
    width = halo + SUB
    k = jnp.concatenate([kh_ref[...], km_ref[...]], axis=0)
    v = jnp.concatenate([vh_ref[...], vm_ref[...]], axis=0)
    c = lax.broadcasted_iota(jnp.int32, (SUB, width), 1)
    if has_bias:
        bias = bias_ref[0]
    else:
        bias = jnp.where(_band_mask(SUB, halo, n_band), 0.0, NEG_INF).astype(F32)
    for sb in range(qb // SUB):
        kw = k[sb * SUB:sb * SUB + width]
        vw = v[sb * SUB:sb * SUB + width]
        first_valid = halo - i * qb - sb * SUB
        for g in range(group):
            q = q_ref[sb * SUB:(sb + 1) * SUB, g * HEAD_DIM:(g + 1) * HEAD_DIM]
            s = _dot_t(q, kw) + bias
            s = jnp.where(c >= first_valid, s, NEG_INF)
            m = jnp.max(s, axis=-1, keepdims=True)
            if has_sink:
                sk = sink_ref[h * group + g]
                m = jnp.maximum(m, sk)
            p = jnp.exp(s - m)
            l = jnp.sum(p, axis=-1, keepdims=True)
            if has_sink:
                l = l + jnp.exp(sk - m)
            o = _dot(p.astype(BF16), vw) / l
            o_ref[sb * SUB:(sb + 1) * SUB, g * HEAD_DIM:(g + 1) * HEAD_DIM] = o.astype(o_ref.dtype)


def band_prompt(proj, *, q_col, k_col, v_col, n_kv, group, halo, n_band, bias=None, sink=None):
    s = proj.shape[0]
    qb = BAND_QB
    assert s % qb == 0 and qb % halo == 0 and halo % CHUNK == 0
    qw = group * HEAD_DIM
    hpb = qb // halo
    qc, kc, vc = q_col // qw, k_col // HEAD_DIM, v_col // HEAD_DIM
    main = lambda c0: pl.BlockSpec((qb, HEAD_DIM), lambda h, i: (i, c0 + h))
    halo_spec = lambda c0: pl.BlockSpec((halo, HEAD_DIM), lambda h, i: (jnp.maximum(i * hpb - 1, 0), c0 + h))
    in_specs = [pl.BlockSpec((qb, qw), lambda h, i: (i, qc + h)), main(kc), halo_spec(kc), main(vc), halo_spec(vc)]
    args = [proj, proj, proj, proj, proj]
    if bias is not None:
        in_specs.append(pl.BlockSpec((1, SUB, halo + SUB), lambda h, i: (h, 0, 0)))
        args.append(bias)
    if sink is not None:
        in_specs.append(pl.BlockSpec(memory_space=pltpu.SMEM))
        args.append(sink)
    return pl.pallas_call(
        functools.partial(_band_prompt_kernel, group=group, halo=halo, n_band=n_band,
                          has_bias=bias is not None, has_sink=sink is not None),
        grid=(n_kv, s // qb),
        in_specs=in_specs,
        out_specs=pl.BlockSpec((qb, qw), lambda h, i: (i, h)),
        out_shape=jax.ShapeDtypeStruct((s, n_kv * qw), BF16),
        compiler_params=_params(("parallel", "arbitrary")),
        name="band_prompt",
    )(*args)


def _band_sample_kernel(*refs, group, has_bias, has_sink):
    q_ref, kn_ref, vn_ref, kc_ref, vc_ref = refs[:5]
    rest = list(refs[5:])
    bias_ref = rest.pop(0) if has_bias else None
    sink_ref = rest.pop(0) if has_sink else None
    o_ref, ko_ref, vo_ref = rest
    h = pl.program_id(1)
    t = q_ref.shape[0]
    lc = kc_ref.shape[0]
    kc = kc_ref[...]
    vc = vc_ref[...]
    kn = kn_ref[...]
    vn = vn_ref[...]
    ko_ref[:lc - t, :] = kc[t:, :]
    ko_ref[lc - t:, :] = kn
    vo_ref[:lc - t, :] = vc[t:, :]
    vo_ref[lc - t:, :] = vn
    q = jnp.concatenate([q_ref[:, g * HEAD_DIM:(g + 1) * HEAD_DIM] for g in range(group)], axis=0).astype(BF16)
    sc = _dot_t(q, kc.astype(BF16))
    sn = _dot_t(q, kn.astype(BF16))
    if has_bias:
        sc = sc + bias_ref[0, :t, :lc]
        sn = sn + bias_ref[0, :t, lc:lc + t]
    m = jnp.maximum(jnp.max(sc, axis=-1, keepdims=True), jnp.max(sn, axis=-1, keepdims=True))
    if has_sink:
        row = lax.broadcasted_iota(jnp.int32, (group * t, 1), 0)
        sk = jnp.zeros((group * t, 1), F32)
        for g in range(group):
            sk = jnp.where((row >= g * t) & (row < (g + 1) * t), sink_ref[h * group + g], sk)
        m = jnp.maximum(m, sk)
    pc = jnp.exp(sc - m)
    pn = jnp.exp(sn - m)
    l = jnp.sum(pc, axis=-1, keepdims=True) + jnp.sum(pn, axis=-1, keepdims=True)
    if has_sink:
        l = l + jnp.exp(sk - m)
    o = (_dot(pc.astype(BF16), vc.astype(BF16)) + _dot(pn.astype(BF16), vn.astype(BF16))) / l
    for g in range(group):
        o_ref[:, g * HEAD_DIM:(g + 1) * HEAD_DIM] = o[g * t:(g + 1) * t].astype(o_ref.dtype)


def band_sample(proj, cache_k, cache_v, *, t, q_col, k_col, v_col, n_kv, group, bias=None, sink=None):
    nb, lc, _ = cache_k.shape
    qw = group * HEAD_DIM
    qc, kc, vc = q_col // qw, k_col // HEAD_DIM, v_col // HEAD_DIM
    new = lambda c0: pl.BlockSpec((t, HEAD_DIM), lambda b, h: (b, c0 + h))
    cache_spec = pl.BlockSpec((None, lc, HEAD_DIM), lambda b, h: (b, 0, h))
    in_specs = [pl.BlockSpec((t, qw), lambda b, h: (b, qc + h)), new(kc), new(vc), cache_spec, cache_spec]
    args = [proj, proj, proj, cache_k, cache_v]
    if bias is not None:
        assert lc == B_REACH and t <= CHUNK
        in_specs.append(pl.BlockSpec((1, SUB, B_REACH + SUB), lambda b, h: (h, 0, 0)))
        args.append(bias)
    if sink is not None:
        in_specs.append(pl.BlockSpec(memory_space=pltpu.SMEM))
        args.append(sink)
    return pl.pallas_call(
        functools.partial(_band_sample_kernel, group=group, has_bias=bias is not None, has_sink=sink is not None),
        grid=(nb, n_kv),
        in_specs=in_specs,
        out_specs=[pl.BlockSpec((t, qw), lambda b, h: (b, h)), cache_spec, cache_spec],
        out_shape=[jax.ShapeDtypeStruct((nb * t, n_kv * qw), BF16),
                   jax.ShapeDtypeStruct(cache_k.shape, F32), jax.ShapeDtypeStruct(cache_v.shape, F32)],
        compiler_params=_params(("parallel", "arbitrary")),
        name="band_sample",
    )(*args)


def _tri2():
    j = lax.broadcasted_iota(jnp.int32, (SB_BLOCK, 2 * SB_BLOCK), 0)
    s = lax.broadcasted_iota(jnp.int32, (SB_BLOCK, 2 * SB_BLOCK), 1)
    return jnp.where((s >= SB_BLOCK) | (j > s), 1.0, 0.0).astype(BF16)


def _neg_log_one_minus_beta(z):
    return jnp.maximum(z, 0.0) + jnp.log1p(jnp.exp(-jnp.abs(z)))


def _sb_diag(q, kb, vb, tri2):
    rows = q.shape[0]
    r = lax.broadcasted_iota(jnp.int32, (rows, SB_BLOCK), 0)
    c = lax.broadcasted_iota(jnp.int32, (rows, SB_BLOCK), 1)
    mask = c < r
    z = _dot_t(q, kb)
    nls = jnp.where(mask, _neg_log_one_minus_beta(z), 0.0)
    la = _dot(nls.astype(BF16), tri2)
    a = jnp.where(mask, jnp.exp(z - nls - la[:, :SB_BLOCK]), 0.0)
    return _dot(a.astype(BF16), vb), la[:, SB_BLOCK:]


def _sb_full(q, kb, vb, tri2, acc, carry):
    z = _dot_t(q, kb)
    nls = _neg_log_one_minus_beta(z)
    la = _dot(nls.astype(BF16), tri2)
    a = jnp.exp(z - nls - la[:, :SB_BLOCK] - carry)
    return acc + _dot(a.astype(BF16), vb), carry + la[:, SB_BLOCK:]


def _sb_block(ref, j):
    return ref[pl.ds(pl.multiple_of(j * SB_BLOCK, SB_BLOCK), SB_BLOCK), :].astype(BF16)


def _sb_older_blocks(q, k_ref, v_ref, tri2, first, acc, carry):
    def cond(state):
        j, _, carry = state
        return (j >= 0) & (jnp.min(carry) < SB_SKIP_LOG)

    def body(state):
        j, acc, carry = state
        acc, carry = _sb_full(q, _sb_block(k_ref, j), _sb_block(v_ref, j), tri2, acc, carry)
        return j - 1, acc, carry

    _, acc, _ = lax.while_loop(cond, body, (first, acc, carry))
    return acc


def _sb_prompt_kernel(q_ref, k_ref, v_ref, o_ref):
    nsub = q_ref.shape[0] // SB_BLOCK
    base = pl.program_id(1) * nsub
    tri2 = _tri2()
    accs, carries = [], []
    for a in range(nsub):
        qa = q_ref[a * SB_BLOCK:(a + 1) * SB_BLOCK, :]
        acc, carry = _sb_diag(qa, _sb_block(k_ref, base + a), _sb_block(v_ref, base + a), tri2)
        for b in range(a - 1, -1, -1):
            acc, carry = _sb_full(qa, _sb_block(k_ref, base + b), _sb_block(v_ref, base + b), tri2, acc, carry)
        accs.append(acc)
        carries.append(carry)
    acc = jnp.concatenate(accs, axis=0)
    carry = jnp.concatenate(carries, axis=0)
    acc = _sb_older_blocks(q_ref[...], k_ref, v_ref, tri2, base - 1, acc, carry)
    o_ref[...] = acc.astype(o_ref.dtype)


def sb_prompt(q, k, v):
    s = q.shape[0]
    tq = min(SB_QUERY_ROWS, s)
    assert s % tq == 0 and tq % SB_BLOCK == 0
    head_all = pl.BlockSpec((s, HEAD_DIM), lambda h, i: (0, h))
    return pl.pallas_call(
        _sb_prompt_kernel,
        grid=(HC, s // tq),
        in_specs=[pl.BlockSpec((tq, HEAD_DIM), lambda h, i: (i, h)), head_all, head_all],
        out_specs=pl.BlockSpec((tq, HEAD_DIM), lambda h, i: (i, h)),
        out_shape=jax.ShapeDtypeStruct(q.shape, BF16),
        compiler_params=_params(("parallel", "arbitrary")),
        name="sb_prompt",
    )(q, k, v)


def _sb_sample_kernel(q_ref, kn_ref, vn_ref, kc_ref, vc_ref, o_ref):
    t = q_ref.shape[0]
    nblk = kc_ref.shape[0] // SB_BLOCK
    q = q_ref[...].astype(BF16)
    tri2 = _tri2()
    zeros = jnp.zeros((SB_BLOCK - t, HEAD_DIM), BF16)
    kb = jnp.concatenate([kn_ref[...].astype(BF16), zeros], axis=0)
    vb = jnp.concatenate([vn_ref[...].astype(BF16), zeros], axis=0)
    acc, carry = _sb_diag(q, kb, vb, tri2)
    acc = _sb_older_blocks(q, kc_ref, vc_ref, tri2, nblk - 1, acc, carry)
    o_ref[...] = acc.astype(o_ref.dtype)


def sb_sample(q, k_new, v_new, cache_k, cache_v, *, t):
    nb, past, _ = cache_k.shape
    assert past % SB_BLOCK == 0 and t <= SB_BLOCK
    new = pl.BlockSpec((t, HEAD_DIM), lambda b, h: (b, h))
    cache_spec = pl.BlockSpec((None, past, HEAD_DIM), lambda b, h: (b, 0, h))
    return pl.pallas_call(
        _sb_sample_kernel,
        grid=(nb, HC),
        in_specs=[new, new, new, cache_spec, cache_spec],
        out_specs=new,
        out_shape=jax.ShapeDtypeStruct(q.shape, BF16),
        compiler_params=_params(("parallel", "arbitrary")),
        name="sb_sample",
    )(q, k_new, v_new, cache_k, cache_v)


def _rope_tables(pos):
    half = HEAD_DIM // 2
    inv = ROPE_THETA ** (-jnp.arange(half, dtype=F32) / half)
    ang = pos.astype(F32)[:, None] * inv[None, :]
    cos, sin = jnp.cos(ang), jnp.sin(ang)
    return jnp.concatenate([cos, cos], axis=-1), jnp.concatenate([-sin, sin], axis=-1)


def kernel(x_prompt, x_sample, cache_a_k, cache_a_v, cache_b_k, cache_b_v, cache_c_k, cache_c_v, norm_mix, w_in_ab, sink_a, rel_bias_b, w_out_ab, w_in_c, w_out_c, norm_ffn, w_gate, w_up, w_down, norm_final):
    bp, seq, d = x_prompt.shape
    nb, t, _ = x_sample.shape
    depth = norm_mix.shape[0]
    past = cache_c_k.shape[2]
    assert bp == 1 and seq >= B_REACH
    keep_a, keep_b = min(A_REACH, seq), min(B_REACH, seq)
    wc = HC * HEAD_DIM

    yp = x_prompt.reshape(seq, d)
    ys = x_sample.reshape(nb * t, d)
    cos_p, sin_p = _rope_tables(jnp.arange(seq))
    cos_s, sin_s = _rope_tables(past + jnp.arange(t))
    cos_s, sin_s = jnp.tile(cos_s, (nb, 1)), jnp.tile(sin_s, (nb, 1))

    outs = {name: [] for name in ("pa_k", "pa_v", "pb_k", "pb_v", "pc_k", "pc_v",
                                  "sa_k", "sa_v", "sb_k", "sb_v", "sc_k", "sc_v")}
    for layer in range(depth):
        i = layer // 2
        hp = rmsnorm(yp, norm_mix[layer], BF16)
        hs = rmsnorm(ys, norm_mix[layer], BF16)
        if layer % 2 == 0:
            w_in = cast_bf16(w_in_ab, i)
            w_out = cast_bf16(w_out_ab, i)
            bias = bias_prep(rel_bias_b[i])
            proj = matmul_ab(hp, w_in, cos_p, sin_p, BF16)
            oa = band_prompt(proj, q_col=0, k_col=AB_SPLITS[0], v_col=AB_SPLITS[1], n_kv=KV_A, group=G_A,
                             halo=A_REACH, n_band=A_BAND_CHUNKS, sink=sink_a[i])
            ob = band_prompt(proj, q_col=AB_SPLITS[2], k_col=AB_SPLITS[3], v_col=AB_SPLITS[4], n_kv=HB, group=1,
                             halo=B_REACH, n_band=B_BAND_CHUNKS, bias=bias)
            yp = matmul2_res(oa, ob, w_out, yp)
            tail = matmul_ab(hp[seq - keep_b:], w_in, cos_p[seq - keep_b:], sin_p[seq - keep_b:], F32)
            outs["pa_k"].append(tail[keep_b - keep_a:, AB_SPLITS[0]:AB_SPLITS[1]].reshape(1, keep_a, KV_A, HEAD_DIM))
            outs["pa_v"].append(tail[keep_b - keep_a:, AB_SPLITS[1]:AB_SPLITS[2]].reshape(1, keep_a, KV_A, HEAD_DIM))
            outs["pb_k"].append(tail[:, AB_SPLITS[3]:AB_SPLITS[4]].reshape(1, keep_b, HB, HEAD_DIM))
            outs["pb_v"].append(tail[:, AB_SPLITS[4]:].reshape(1, keep_b, HB, HEAD_DIM))
            projs = matmul_ab(hs, w_in, cos_s, sin_s, F32)
            la, lb = cache_a_k.shape[2], cache_b_k.shape[2]
            oa, nak, nav = band_sample(projs, cache_a_k[i].reshape(nb, la, KA_W), cache_a_v[i].reshape(nb, la, KA_W),
                                       t=t, q_col=0, k_col=AB_SPLITS[0], v_col=AB_SPLITS[1], n_kv=KV_A, group=G_A,
                                       sink=sink_a[i])
            ob, nbk, nbv = band_sample(projs, cache_b_k[i].reshape(nb, lb, QB_W), cache_b_v[i].reshape(nb, lb, QB_W),
                                       t=t, q_col=AB_SPLITS[2], k_col=AB_SPLITS[3], v_col=AB_SPLITS[4], n_kv=HB,
                                       group=1, bias=bias)
            ys = matmul2_res(oa, ob, w_out, ys)
            outs["sa_k"].append(nak.reshape(nb, la, KV_A, HEAD_DIM))
            outs["sa_v"].append(nav.reshape(nb, la, KV_A, HEAD_DIM))
            outs["sb_k"].append(nbk.reshape(nb, lb, HB, HEAD_DIM))
            outs["sb_v"].append(nbv.reshape(nb, lb, HB, HEAD_DIM))
        else:
            w_in = cast_bf16(w_in_c, i)
            w_out = cast_bf16(w_out_c, i)
            q = matmul(hp, w_in, BF16, scale=QK_SCALE, col0=0, n=wc)
            k32, k16 = matmul_dual(hp, w_in, col0=wc, n=wc)
            v32, v16 = matmul_dual(hp, w_in, col0=2 * wc, n=wc)
            att = sb_prompt(q, k16, v16)
            yp = matmul_res(att, w_out, yp, tm=1024, tn=512)
            outs["pc_k"].append(k32.reshape(1, seq, HC, HEAD_DIM))
            outs["pc_v"].append(v32.reshape(1, seq, HC, HEAD_DIM))
            qs = matmul(hs, w_in, F32, scale=QK_SCALE, col0=0, n=wc)
            ks = matmul(hs, w_in, F32, col0=wc, n=wc)
            vs = matmul(hs, w_in, F32, col0=2 * wc, n=wc)
            att = sb_sample(qs, ks, vs, cache_c_k[i].reshape(nb, past, wc), cache_c_v[i].reshape(nb, past, wc), t=t)
            ys = matmul_res(att, w_out, ys, tm=1024, tn=512)
            outs["sc_k"].append(ks.reshape(nb, t, HC, HEAD_DIM))
            outs["sc_v"].append(vs.reshape(nb, t, HC, HEAD_DIM))
        wg, wu, wd = cast_bf16(w_gate, layer), cast_bf16(w_up, layer), cast_bf16(w_down, layer)
        yp = matmul_res(gateup(rmsnorm(yp, norm_ffn[layer], BF16), wg, wu), wd, yp)
        ys = matmul_res(gateup(rmsnorm(ys, norm_ffn[layer], BF16), wg, wu), wd, ys)

    y_prompt = rmsnorm(yp, norm_final, F32).reshape(1, seq, d)
    y_sample = rmsnorm(ys, norm_final, F32).reshape(nb, t, d)
    st = {name: jnp.stack(v) for name, v in outs.items()}
    return (y_prompt, y_sample, st["pa_k"], st["pa_v"], st["pb_k"], st["pb_v"], st["pc_k"], st["pc_v"],
            st["sa_k"], st["sa_v"], st["sb_k"], st["sb_v"], st["sc_k"], st["sc_v"])
```

```python
import functools

import jax
import jax.numpy as jnp
from jax import lax
from jax.experimental import pallas as pl
from jax.experimental.pallas import tpu as pltpu

HEAD_DIM = 128
CHUNK = 64
HA = 16
KV_A = 4
G_A = HA // KV_A
HB = 16
HC = 16
A_BAND_CHUNKS = 3
B_BAND_CHUNKS = 9
A_REACH = (A_BAND_CHUNKS - 1) * CHUNK
B_REACH = (B_BAND_CHUNKS - 1) * CHUNK
REL_CLIP = 256
N_REL = 2 * REL_CLIP + 1
SB_BLOCK = 128
ROPE_THETA = 10000.0
RMS_EPS = 1e-6
NEG_INF = -1e30
QA_W = HA * HEAD_DIM
KA_W = KV_A * HEAD_DIM
QB_W = HB * HEAD_DIM
AB_SPLITS = (QA_W, QA_W + KA_W, QA_W + 2 * KA_W, QA_W + 2 * KA_W + QB_W, QA_W + 2 * KA_W + 2 * QB_W)
QK_SCALE = HEAD_DIM ** -0.5

SB_QUERY_ROWS = 512
SB_LOOKBACK = 2
SB_SKIP_LOG = 106.0
SUB = 128
BAND_QB = 512
VMEM_LIMIT_MB = 56

F32 = jnp.float32
BF16 = jnp.bfloat16


def _params(semantics):
    return pltpu.CompilerParams(dimension_semantics=semantics, vmem_limit_bytes=VMEM_LIMIT_MB << 20)


def _dot(a, b):
    return jnp.dot(a, b, preferred_element_type=F32)


def _dot_t(a, b):
    return lax.dot_general(a, b, (((1,), (1,)), ((), ())), preferred_element_type=F32)


def _rmsnorm_kernel(x_ref, g_ref, o_ref):
    x = x_ref[...]
    ms = jnp.mean(x * x, axis=-1, keepdims=True)
    o_ref[...] = (x * lax.rsqrt(ms + RMS_EPS) * g_ref[...]).astype(o_ref.dtype)


def rmsnorm(x, g, out_dtype):
    m, d = x.shape
    tm = min(256, m)
    return pl.pallas_call(
        _rmsnorm_kernel,
        grid=(m // tm,),
        in_specs=[pl.BlockSpec((tm, d), lambda i: (i, 0)), pl.BlockSpec((1, d), lambda i: (0, 0))],
        out_specs=pl.BlockSpec((tm, d), lambda i: (i, 0)),
        out_shape=jax.ShapeDtypeStruct((m, d), out_dtype),
        compiler_params=_params(("parallel",)),
        name="rmsnorm",
    )(x, g.reshape(1, d))


def _cast_kernel(x_ref, o_ref):
    o_ref[...] = x_ref[...].astype(o_ref.dtype)


def cast_bf16(w_stack, layer):
    _, k, n = w_stack.shape
    tr = min(256, k)
    assert k % tr == 0
    return pl.pallas_call(
        _cast_kernel,
        grid=(k // tr,),
        in_specs=[pl.BlockSpec((None, tr, n), lambda i: (layer, i, 0))],
        out_specs=pl.BlockSpec((tr, n), lambda i: (i, 0)),
        out_shape=jax.ShapeDtypeStruct((k, n), BF16),
        compiler_params=_params(("parallel",)),
        name="cast_bf16",
    )(w_stack)


def _mm_kernel(a_ref, w_ref, o_ref, *, scale):
    acc = _dot(a_ref[...], w_ref[...])
    if scale is not None:
        acc = acc * scale
    o_ref[...] = acc.astype(o_ref.dtype)


def _mm_dual_kernel(a_ref, w_ref, o32_ref, o16_ref):
    acc = _dot(a_ref[...], w_ref[...])
    o32_ref[...] = acc
    o16_ref[...] = acc.astype(o16_ref.dtype)


def _mm_res_kernel(a_ref, w_ref, r_ref, o_ref):
    o_ref[...] = r_ref[...] + _dot(a_ref[...], w_ref[...])


def _mm2_res_kernel(a1_ref, a2_ref, w_ref, r_ref, o_ref):
    k1 = a1_ref.shape[1]
    acc = _dot(a1_ref[...], w_ref[:k1, :]) + _dot(a2_ref[...], w_ref[k1:, :])
    o_ref[...] = r_ref[...] + acc


def _gateup_kernel(a_ref, wg_ref, wu_ref, o_ref):
    a = a_ref[...]
    g = _dot(a, wg_ref[...])
    u = _dot(a, wu_ref[...])
    o_ref[...] = (g / (1.0 + jnp.exp(-g)) * u).astype(o_ref.dtype)


def _mm_ab_kernel(a_ref, w_ref, cos_ref, sin_ref, o_ref, *, tn):
    col0 = pl.program_id(1) * tn
    acc = _dot(a_ref[...], w_ref[...])
    is_rope = col0 < AB_SPLITS[1]
    is_q = (col0 < AB_SPLITS[0]) | ((col0 >= AB_SPLITS[2]) & (col0 < AB_SPLITS[3]))
    s = jnp.where(is_q, QK_SCALE, 1.0).astype(F32)

    @pl.when(is_rope)
    def _():
        cos = cos_ref[...]
        sin = sin_ref[...]
        for g in range(tn // HEAD_DIM):
            blk = acc[:, g * HEAD_DIM:(g + 1) * HEAD_DIM]
            rot = blk * cos + pltpu.roll(blk, HEAD_DIM // 2, axis=1) * sin
            o_ref[:, g * HEAD_DIM:(g + 1) * HEAD_DIM] = (rot * s).astype(o_ref.dtype)

    @pl.when(jnp.logical_not(is_rope))
    def _():
        o_ref[...] = (acc * s).astype(o_ref.dtype)


def _tiles(m, n, tm, tn):
    tm = min(tm, m)
    tn = min(tn, n)
    assert m % tm == 0 and n % tn == 0, (m, n, tm, tn)
    return tm, tn


def matmul(a, w, out_dtype, *, scale=None, col0=0, n=None, tm=1024, tn=512):
    m, k = a.shape
    n = w.shape[1] - col0 if n is None else n
    tm, tn = _tiles(m, n, tm, tn)
    assert col0 % tn == 0
    c0 = col0 // tn
    return pl.pallas_call(
        functools.partial(_mm_kernel, scale=scale),
        grid=(m // tm, n // tn),
        in_specs=[pl.BlockSpec((tm, k), lambda i, j: (i, 0)), pl.BlockSpec((k, tn), lambda i, j: (0, c0 + j))],
        out_specs=pl.BlockSpec((tm, tn), lambda i, j: (i, j)),
        out_shape=jax.ShapeDtypeStruct((m, n), out_dtype),
        compiler_params=_params(("parallel", "arbitrary")),
        name="matmul",
    )(a, w)


def matmul_dual(a, w, *, col0, n, tm=1024, tn=512):
    m, k = a.shape
    tm, tn = _tiles(m, n, tm, tn)
    assert col0 % tn == 0
    c0 = col0 // tn
    return pl.pallas_call(
        _mm_dual_kernel,
        grid=(m // tm, n // tn),
        in_specs=[pl.BlockSpec((tm, k), lambda i, j: (i, 0)), pl.BlockSpec((k, tn), lambda i, j: (0, c0 + j))],
        out_specs=[pl.BlockSpec((tm, tn), lambda i, j: (i, j)), pl.BlockSpec((tm, tn), lambda i, j: (i, j))],
        out_shape=[jax.ShapeDtypeStruct((m, n), F32), jax.ShapeDtypeStruct((m, n), BF16)],
        compiler_params=_params(("parallel", "arbitrary")),
        name="matmul_dual",
    )(a, w)


def matmul_res(a, w, res, *, tm=512, tn=512):
    m, k = a.shape
    n = w.shape[1]
    tm, tn = _tiles(m, n, tm, tn)
    return pl.pallas_call(
        _mm_res_kernel,
        grid=(m // tm, n // tn),
        in_specs=[pl.BlockSpec((tm, k), lambda i, j: (i, 0)), pl.BlockSpec((k, tn), lambda i, j: (0, j)),
                  pl.BlockSpec((tm, tn), lambda i, j: (i, j))],
        out_specs=pl.BlockSpec((tm, tn), lambda i, j: (i, j)),
        out_shape=jax.ShapeDtypeStruct((m, n), F32),
        compiler_params=_params(("parallel", "arbitrary")),
        name="matmul_res",
    )(a, w, res)


def matmul2_res(a1, a2, w, res, *, tm=1024, tn=512):
    m, k1 = a1.shape
    k2 = a2.shape[1]
    n = w.shape[1]
    tm, tn = _tiles(m, n, tm, tn)
    return pl.pallas_call(
        _mm2_res_kernel,
        grid=(m // tm, n // tn),
        in_specs=[pl.BlockSpec((tm, k1), lambda i, j: (i, 0)), pl.BlockSpec((tm, k2), lambda i, j: (i, 0)),
                  pl.BlockSpec((k1 + k2, tn), lambda i, j: (0, j)), pl.BlockSpec((tm, tn), lambda i, j: (i, j))],
        out_specs=pl.BlockSpec((tm, tn), lambda i, j: (i, j)),
        out_shape=jax.ShapeDtypeStruct((m, n), F32),
        compiler_params=_params(("parallel", "arbitrary")),
        name="matmul2_res",
    )(a1, a2, w, res)


def gateup(a, wg, wu, *, tm=2048, tn=256):
    m, k = a.shape
    n = wg.shape[1]
    tm, tn = _tiles(m, n, tm, tn)
    return pl.pallas_call(
        _gateup_kernel,
        grid=(m // tm, n // tn),
        in_specs=[pl.BlockSpec((tm, k), lambda i, j: (i, 0)), pl.BlockSpec((k, tn), lambda i, j: (0, j)),
                  pl.BlockSpec((k, tn), lambda i, j: (0, j))],
        out_specs=pl.BlockSpec((tm, tn), lambda i, j: (i, j)),
        out_shape=jax.ShapeDtypeStruct((m, n), BF16),
        compiler_params=_params(("parallel", "arbitrary")),
        name="gateup",
    )(a, wg, wu)


def matmul_ab(a, w, cos, sin, out_dtype, *, tm=1024):
    m, k = a.shape
    n = w.shape[1]
    tn = 512
    assert all(s % tn == 0 for s in AB_SPLITS)
    tm, tn = _tiles(m, n, tm, tn)
    return pl.pallas_call(
        functools.partial(_mm_ab_kernel, tn=tn),
        grid=(m // tm, n // tn),
        in_specs=[pl.BlockSpec((tm, k), lambda i, j: (i, 0)), pl.BlockSpec((k, tn), lambda i, j: (0, j)),
                  pl.BlockSpec((tm, HEAD_DIM), lambda i, j: (i, 0)), pl.BlockSpec((tm, HEAD_DIM), lambda i, j: (i, 0))],
        out_specs=pl.BlockSpec((tm, tn), lambda i, j: (i, j)),
        out_shape=jax.ShapeDtypeStruct((m, n), out_dtype),
        compiler_params=_params(("parallel", "arbitrary")),
        name="matmul_ab",
    )(a, w, cos, sin)


def _band_mask(rows, halo, n_band):
    r = lax.broadcasted_iota(jnp.int32, (rows, halo + SUB), 0)
    c = lax.broadcasted_iota(jnp.int32, (rows, halo + SUB), 1)
    rc = (r + halo) >> 6
    cc = c >> 6
    return (cc <= rc) & (cc > rc - n_band)


def _bias_prep_kernel(tab_ref, o_ref):
    h = pl.program_id(0)
    width = B_REACH + SUB
    ulen = width + SUB
    n = lax.broadcasted_iota(jnp.int32, (8, ulen), 1)
    idx = jnp.clip(B_REACH + SUB - 1 - n, -REL_CLIP, REL_CLIP) + REL_CLIP

    def body(m, u):
        return jnp.where(idx == m, tab_ref[h, m], u)

    u = lax.fori_loop(0, N_REL, body, jnp.zeros((8, ulen), F32))
    x = jnp.broadcast_to(u[0:1, :], (SUB, ulen))
    x = pltpu.roll(x, ulen - (SUB - 1), axis=1, stride=1, stride_axis=0)
    o_ref[0] = jnp.where(_band_mask(SUB, B_REACH, B_BAND_CHUNKS), x[:, :width], NEG_INF)


def bias_prep(table):
    return pl.pallas_call(
        _bias_prep_kernel,
        grid=(HB,),
        in_specs=[pl.BlockSpec(memory_space=pltpu.SMEM)],
        out_specs=pl.BlockSpec((1, SUB, B_REACH + SUB), lambda h: (h, 0, 0)),
        out_shape=jax.ShapeDtypeStruct((HB, SUB, B_REACH + SUB), F32),
        compiler_params=_params(("arbitrary",)),
        name="bias_prep",
    )(table)


def _band_prompt_kernel(*refs, group, halo, n_band, has_bias, has_sink):
    q_ref, km_ref, kh_ref, vm_ref, vh_ref = refs[:5]
    rest = list(refs[5:])
    bias_ref = rest.pop(0) if has_bias else None
    sink_ref = rest.pop(0) if has_sink else None
    o_ref = rest.pop(0)
    h = pl.program_id(0)
    i = pl.program_id(1)
    qb = q_ref.shape[0]
    width = halo + SUB
    k = jnp.concatenate([kh_ref[...], km_ref[...]], axis=0)
    v = jnp.concatenate([vh_ref[...], vm_ref[...]], axis=0)
    c = lax.broadcasted_iota(jnp.int32, (SUB, width), 1)
    if has_bias:
        bias = bias_ref[0]
    else:
        bias = jnp.where(_band_mask(SUB, halo, n_band), 0.0, NEG_INF).astype(F32)
    for sb in range(qb // SUB):
        kw = k[sb * SUB:sb * SUB + width]
        vw = v[sb * SUB:sb * SUB + width]
        first_valid = halo - i * qb - sb * SUB
        for g in range(group):
            q = q_ref[sb * SUB:(sb + 1) * SUB, g * HEAD_DIM:(g + 1) * HEAD_DIM]
            s = _dot_t(q, kw) + bias
            s = jnp.where(c >= first_valid, s, NEG_INF)
            m = jnp.max(s, axis=-1, keepdims=True)
            if has_sink:
                sk = sink_ref[h * group + g]
                m = jnp.maximum(m, sk)
            p = jnp.exp(s - m)
            l = jnp.sum(p, axis=-1, keepdims=True)
            if has_sink:
                l = l + jnp.exp(sk - m)
            o = _dot(p.astype(BF16), vw) / l
            o_ref[sb * SUB:(sb + 1) * SUB, g * HEAD_DIM:(g + 1) * HEAD_DIM] = o.astype(o_ref.dtype)


def band_prompt(proj, *, q_col, k_col, v_col, n_kv, group, halo, n_band, bias=None, sink=None):
    s = proj.shape[0]
    qb = BAND_QB
    assert s % qb == 0 and qb % halo == 0 and halo % CHUNK == 0
    qw = group * HEAD_DIM
    hpb = qb // halo
    qc, kc, vc = q_col // qw, k_col // HEAD_DIM, v_col // HEAD_DIM
    main = lambda c0: pl.BlockSpec((qb, HEAD_DIM), lambda h, i: (i, c0 + h))
    halo_spec = lambda c0: pl.BlockSpec((halo, HEAD_DIM), lambda h, i: (jnp.maximum(i * hpb - 1, 0), c0 + h))
    in_specs = [pl.BlockSpec((qb, qw), lambda h, i: (i, qc + h)), main(kc), halo_spec(kc), main(vc), halo_spec(vc)]
    args = [proj, proj, proj, proj, proj]
    if bias is not None:
        in_specs.append(pl.BlockSpec((1, SUB, halo + SUB), lambda h, i: (h, 0, 0)))
        args.append(bias)
    if sink is not None:
        in_specs.append(pl.BlockSpec(memory_space=pltpu.SMEM))
        args.append(sink)
    return pl.pallas_call(
        functools.partial(_band_prompt_kernel, group=group, halo=halo, n_band=n_band,
                          has_bias=bias is not None, has_sink=sink is not None),
        grid=(n_kv, s // qb),
        in_specs=in_specs,
        out_specs=pl.BlockSpec((qb, qw), lambda h, i: (i, h)),
        out_shape=jax.ShapeDtypeStruct((s, n_kv * qw), BF16),
        compiler_params=_params(("parallel", "arbitrary")),
        name="band_prompt",
    )(*args)


def _band_sample_kernel(*refs, group, has_bias, has_sink):
    q_ref, kn_ref, vn_ref, kc_ref, vc_ref = refs[:5]
    rest = list(refs[5:])
    bias_ref = rest.pop(0) if has_bias else None
    sink_ref = rest.pop(0) if has_sink else None
    o_ref, ko_ref, vo_ref = rest
    h = pl.program_id(1)
    t = q_ref.shape[0]
    lc = kc_ref.shape[0]
    kc = kc_ref[...]
    vc = vc_ref[...]
    kn = kn_ref[...]
    vn = vn_ref[...]
    ko_ref[:lc - t, :] = kc[t:, :]
    ko_ref[lc - t:, :] = kn
    vo_ref[:lc - t, :] = vc[t:, :]
    vo_ref[lc - t:, :] = vn
    q = jnp.concatenate([q_ref[:, g * HEAD_DIM:(g + 1) * HEAD_DIM] for g in range(group)], axis=0).astype(BF16)
    sc = _dot_t(q, kc.astype(BF16))
    sn = _dot_t(q, kn.astype(BF16))
    if has_bias:
        sc = sc + bias_ref[0, :t, :lc]
        sn = sn + bias_ref[0, :t, lc:lc + t]
    m = jnp.maximum(jnp.max(sc, axis=-1, keepdims=True), jnp.max(sn, axis=-1, keepdims=True))
    if has_sink:
        row = lax.broadcasted_iota(jnp.int32, (group * t, 1), 0)
        sk = jnp.zeros((group * t, 1), F32)
        for g in range(group):
            sk = jnp.where((row >= g * t) & (row < (g + 1) * t), sink_ref[h * group + g], sk)
        m = jnp.maximum(m, sk)
    pc = jnp.exp(sc - m)
    pn = jnp.exp(sn - m)
    l = jnp.sum(pc, axis=-1, keepdims=True) + jnp.sum(pn, axis=-1, keepdims=True)
    if has_sink:
        l = l + jnp.exp(sk - m)
    o = (_dot(pc.astype(BF16), vc.astype(BF16)) + _dot(pn.astype(BF16), vn.astype(BF16))) / l
    for g in range(group):
        o_ref[:, g * HEAD_DIM:(g + 1) * HEAD_DIM] = o[g * t:(g + 1) * t].astype(o_ref.dtype)


def band_sample(proj, cache_k, cache_v, *, t, q_col, k_col, v_col, n_kv, group, bias=None, sink=None):
    nb, lc, _ = cache_k.shape
    qw = group * HEAD_DIM
    qc, kc, vc = q_col // qw, k_col // HEAD_DIM, v_col // HEAD_DIM
    new = lambda c0: pl.BlockSpec((t, HEAD_DIM), lambda b, h: (b, c0 + h))
    cache_spec = pl.BlockSpec((None, lc, HEAD_DIM), lambda b, h: (b, 0, h))
    in_specs = [pl.BlockSpec((t, qw), lambda b, h: (b, qc + h)), new(kc), new(vc), cache_spec, cache_spec]
    args = [proj, proj, proj, cache_k, cache_v]
    if bias is not None:
        assert lc == B_REACH and t <= CHUNK
        in_specs.append(pl.BlockSpec((1, SUB, B_REACH + SUB), lambda b, h: (h, 0, 0)))
        args.append(bias)
    if sink is not None:
        in_specs.append(pl.BlockSpec(memory_space=pltpu.SMEM))
        args.append(sink)
    return pl.pallas_call(
        functools.partial(_band_sample_kernel, group=group, has_bias=bias is not None, has_sink=sink is not None),
        grid=(nb, n_kv),
        in_specs=in_specs,
        out_specs=[pl.BlockSpec((t, qw), lambda b, h: (b, h)), cache_spec, cache_spec],
        out_shape=[jax.ShapeDtypeStruct((nb * t, n_kv * qw), BF16),
                   jax.ShapeDtypeStruct(cache_k.shape, F32), jax.ShapeDtypeStruct(cache_v.shape, F32)],
        compiler_params=_params(("parallel", "arbitrary")),
        name="band_sample",
    )(*args)


def _tri2():
    j = lax.broadcasted_iota(jnp.int32, (SB_BLOCK, 2 * SB_BLOCK), 0)
    s = lax.broadcasted_iota(jnp.int32, (SB_BLOCK, 2 * SB_BLOCK), 1)
    return jnp.where((s >= SB_BLOCK) | (j > s), 1.0, 0.0).astype(BF16)


def _softplus(z):
    neg_abs = lax.bitcast_convert_type(lax.bitcast_convert_type(z, jnp.int32) | jnp.int32(-2 ** 31), F32)
    return jnp.maximum(z, 0.0) + jnp.log(1.0 + jnp.exp(neg_abs))


def _sb_tile(z, tri2, carry):
    nls = _softplus(z)
    la = _dot(nls.astype(BF16), tri2)
    logw = z - nls - la[:, :SB_BLOCK]
    if carry is None:
        return logw, la[:, SB_BLOCK:]
    return logw - carry, carry + la[:, SB_BLOCK:]


def _causal_tile_mask(rows):
    r = lax.broadcasted_iota(jnp.int32, (rows, SB_BLOCK), 0)
    c = lax.broadcasted_iota(jnp.int32, (rows, SB_BLOCK), 1)
    return c < r


def _sb_block(ref, j):
    return ref[pl.ds(pl.multiple_of(j * SB_BLOCK, SB_BLOCK), SB_BLOCK), :].astype(BF16)


def _sb_older_blocks(q, k_ref, v_ref, tri2, first, acc, carry):
    def cond(state):
        j, _, carry = state
        return (j >= 0) & (jnp.min(carry) < SB_SKIP_LOG)

    def body(state):
        j, acc, carry = state
        logw, carry = _sb_tile(_dot_t(q, _sb_block(k_ref, j)), tri2, carry)
        acc = acc + _dot(jnp.exp(logw).astype(BF16), _sb_block(v_ref, j))
        return j - 1, acc, carry

    _, acc, _ = lax.while_loop(cond, body, (first, acc, carry))
    return acc


def _sb_prompt_kernel(q_ref, k_ref, v_ref, o_ref):
    tq = q_ref.shape[0]
    nsub = tq // SB_BLOCK
    nwin = nsub + SB_LOOKBACK
    pid = pl.program_id(1)
    base = pid * nsub
    is_first = pid == 0
    start = pl.multiple_of(jnp.maximum(base - SB_LOOKBACK, 0) * SB_BLOCK, SB_BLOCK)
    q = q_ref[...]
    kw = k_ref[pl.ds(start, nwin * SB_BLOCK), :]
    vw = v_ref[pl.ds(start, nwin * SB_BLOCK), :]
    kw = jnp.where(is_first, jnp.roll(kw, SB_LOOKBACK * SB_BLOCK, axis=0), kw)
    vw = jnp.where(is_first, jnp.roll(vw, SB_LOOKBACK * SB_BLOCK, axis=0), vw)
    tri2 = _tri2()
    z = _dot_t(q, kw)
    diag = _causal_tile_mask(SB_BLOCK)
    accs, carries = [], []
    for a in range(nsub):
        rows = slice(a * SB_BLOCK, (a + 1) * SB_BLOCK)
        carry = None
        pieces = []
        for b in range(a + SB_LOOKBACK, -1, -1):
            zt = z[rows, b * SB_BLOCK:(b + 1) * SB_BLOCK]
            if b == a + SB_LOOKBACK:
                zt = jnp.where(diag, zt, NEG_INF)
            if b < SB_LOOKBACK:
                zt = jnp.where(is_first, NEG_INF, zt)
            logw, carry = _sb_tile(zt, tri2, carry)
            pieces.insert(0, logw)
        w = jnp.exp(jnp.concatenate(pieces, axis=1)).astype(BF16)
        accs.append(_dot(w, vw[:(a + SB_LOOKBACK + 1) * SB_BLOCK]))
        carries.append(carry)
    acc = jnp.concatenate(accs, axis=0)
    carry = jnp.concatenate(carries, axis=0)
    acc = _sb_older_blocks(q, k_ref, v_ref, tri2, base - SB_LOOKBACK - 1, acc, carry)
    o_ref[...] = acc.astype(o_ref.dtype)


def sb_prompt(q, k, v):
    s = q.shape[0]
    tq = min(SB_QUERY_ROWS, s)
    assert s % tq == 0 and tq % SB_BLOCK == 0 and s >= tq + SB_LOOKBACK * SB_BLOCK
    head_all = pl.BlockSpec((s, HEAD_DIM), lambda h, i: (0, h))
    return pl.pallas_call(
        _sb_prompt_kernel,
        grid=(HC, s // tq),
        in_specs=[pl.BlockSpec((tq, HEAD_DIM), lambda h, i: (i, h)), head_all, head_all],
        out_specs=pl.BlockSpec((tq, HEAD_DIM), lambda h, i: (i, h)),
        out_shape=jax.ShapeDtypeStruct(q.shape, BF16),
        compiler_params=_params(("parallel", "arbitrary")),
        name="sb_prompt",
    )(q, k, v)


def _sb_sample_kernel(q_ref, kn_ref, vn_ref, kc_ref, vc_ref, o_ref):
    t = q_ref.shape[0]
    nblk = kc_ref.shape[0] // SB_BLOCK
    q = q_ref[...].astype(BF16)
    tri2 = _tri2()
    zeros = jnp.zeros((SB_BLOCK - t, HEAD_DIM), BF16)
    kb = jnp.concatenate([kn_ref[...].astype(BF16), zeros], axis=0)
    vb = jnp.concatenate([vn_ref[...].astype(BF16), zeros], axis=0)
    logw, carry = _sb_tile(jnp.where(_causal_tile_mask(t), _dot_t(q, kb), NEG_INF), tri2, None)
    acc = _dot(jnp.exp(logw).astype(BF16), vb)
    acc = _sb_older_blocks(q, kc_ref, vc_ref, tri2, nblk - 1, acc, carry)
    o_ref[...] = acc.astype(o_ref.dtype)


def sb_sample(q, k_new, v_new, cache_k, cache_v, *, t):
    nb, past, _ = cache_k.shape
    assert past % SB_BLOCK == 0 and t <= SB_BLOCK
    new = pl.BlockSpec((t, HEAD_DIM), lambda b, h: (b, h))
    cache_spec = pl.BlockSpec((None, past, HEAD_DIM), lambda b, h: (b, 0, h))
    return pl.pallas_call(
        _sb_sample_kernel,
        grid=(nb, HC),
        in_specs=[new, new, new, cache_spec, cache_spec],
        out_specs=new,
        out_shape=jax.ShapeDtypeStruct(q.shape, BF16),
        compiler_params=_params(("parallel", "arbitrary")),
        name="sb_sample",
    )(q, k_new, v_new, cache_k, cache_v)


def _rope_table_kernel(inv_ref, cos_ref, sin_ref, *, pos0):
    tm = cos_ref.shape[0]
    pos = lax.broadcasted_iota(jnp.int32, (tm, HEAD_DIM), 0) + (pos0 + pl.program_id(0) * tm)
    lane = lax.broadcasted_iota(jnp.int32, (tm, HEAD_DIM), 1)
    ang = pos.astype(F32) * inv_ref[...]
    sin = jnp.sin(ang)
    cos_ref[...] = jnp.cos(ang)
    sin_ref[...] = jnp.where(lane < HEAD_DIM // 2, -sin, sin)


def _rope_tables(n, pos0):
    half = HEAD_DIM // 2
    inv = ROPE_THETA ** (-jnp.arange(half, dtype=F32) / half)
    inv = jnp.concatenate([inv, inv]).reshape(1, HEAD_DIM)
    tm = min(1024, n)
    assert n % tm == 0
    return pl.pallas_call(
        functools.partial(_rope_table_kernel, pos0=pos0),
        grid=(n // tm,),
        in_specs=[pl.BlockSpec((1, HEAD_DIM), lambda i: (0, 0))],
        out_specs=[pl.BlockSpec((tm, HEAD_DIM), lambda i: (i, 0))] * 2,
        out_shape=[jax.ShapeDtypeStruct((n, HEAD_DIM), F32)] * 2,
        compiler_params=_params(("parallel",)),
        name="rope_tables",
    )(inv)


def kernel(x_prompt, x_sample, cache_a_k, cache_a_v, cache_b_k, cache_b_v, cache_c_k, cache_c_v, norm_mix, w_in_ab, sink_a, rel_bias_b, w_out_ab, w_in_c, w_out_c, norm_ffn, w_gate, w_up, w_down, norm_final):
    bp, seq, d = x_prompt.shape
    nb, t, _ = x_sample.shape
    depth = norm_mix.shape[0]
    past = cache_c_k.shape[2]
    assert bp == 1 and seq >= B_REACH
    keep_a, keep_b = min(A_REACH, seq), min(B_REACH, seq)
    wc = HC * HEAD_DIM

    yp = x_prompt.reshape(seq, d)
    ys = x_sample.reshape(nb * t, d)
    cos_p, sin_p = _rope_tables(seq, 0)
    cos_s, sin_s = _rope_tables(t, past)
    cos_s, sin_s = jnp.tile(cos_s, (nb, 1)), jnp.tile(sin_s, (nb, 1))

    outs = {name: [] for name in ("pa_k", "pa_v", "pb_k", "pb_v", "pc_k", "pc_v",
                                  "sa_k", "sa_v", "sb_k", "sb_v", "sc_k", "sc_v")}
    for layer in range(depth):
        i = layer // 2
        hp = rmsnorm(yp, norm_mix[layer], BF16)
        hs = rmsnorm(ys, norm_mix[layer], BF16)
        if layer % 2 == 0:
            w_in = cast_bf16(w_in_ab, i)
            w_out = cast_bf16(w_out_ab, i)
            bias = bias_prep(rel_bias_b[i])
            proj = matmul_ab(hp, w_in, cos_p, sin_p, BF16)
            oa = band_prompt(proj, q_col=0, k_col=AB_SPLITS[0], v_col=AB_SPLITS[1], n_kv=KV_A, group=G_A,
                             halo=A_REACH, n_band=A_BAND_CHUNKS, sink=sink_a[i])
            ob = band_prompt(proj, q_col=AB_SPLITS[2], k_col=AB_SPLITS[3], v_col=AB_SPLITS[4], n_kv=HB, group=1,
                             halo=B_REACH, n_band=B_BAND_CHUNKS, bias=bias)
            yp = matmul2_res(oa, ob, w_out, yp)
            tail = matmul_ab(hp[seq - keep_b:], w_in, cos_p[seq - keep_b:], sin_p[seq - keep_b:], F32)
            outs["pa_k"].append(tail[keep_b - keep_a:, AB_SPLITS[0]:AB_SPLITS[1]].reshape(1, keep_a, KV_A, HEAD_DIM))
            outs["pa_v"].append(tail[keep_b - keep_a:, AB_SPLITS[1]:AB_SPLITS[2]].reshape(1, keep_a, KV_A, HEAD_DIM))
            outs["pb_k"].append(tail[:, AB_SPLITS[3]:AB_SPLITS[4]].reshape(1, keep_b, HB, HEAD_DIM))
            outs["pb_v"].append(tail[:, AB_SPLITS[4]:].reshape(1, keep_b, HB, HEAD_DIM))
            projs = matmul_ab(hs, w_in, cos_s, sin_s, F32)
            la, lb = cache_a_k.shape[2], cache_b_k.shape[2]
            oa, nak, nav = band_sample(projs, cache_a_k[i].reshape(nb, la, KA_W), cache_a_v[i].reshape(nb, la, KA_W),
                                       t=t, q_col=0, k_col=AB_SPLITS[0], v_col=AB_SPLITS[1], n_kv=KV_A, group=G_A,
                                       sink=sink_a[i])
            ob, nbk, nbv = band_sample(projs, cache_b_k[i].reshape(nb, lb, QB_W), cache_b_v[i].reshape(nb, lb, QB_W),
                                       t=t, q_col=AB_SPLITS[2], k_col=AB_SPLITS[3], v_col=AB_SPLITS[4], n_kv=HB,
                                       group=1, bias=bias)
            ys = matmul2_res(oa, ob, w_out, ys)
            outs["sa_k"].append(nak.reshape(nb, la, KV_A, HEAD_DIM))
            outs["sa_v"].append(nav.reshape(nb, la, KV_A, HEAD_DIM))
            outs["sb_k"].append(nbk.reshape(nb, lb, HB, HEAD_DIM))
            outs["sb_v"].append(nbv.reshape(nb, lb, HB, HEAD_DIM))
        else:
            w_in = cast_bf16(w_in_c, i)
            w_out = cast_bf16(w_out_c, i)
            q = matmul(hp, w_in, BF16, scale=QK_SCALE, col0=0, n=wc)
            k32, k16 = matmul_dual(hp, w_in, col0=wc, n=wc)
            v32, v16 = matmul_dual(hp, w_in, col0=2 * wc, n=wc)
            att = sb_prompt(q, k16, v16)
            yp = matmul_res(att, w_out, yp, tm=1024, tn=512)
            outs["pc_k"].append(k32.reshape(1, seq, HC, HEAD_DIM))
            outs["pc_v"].append(v32.reshape(1, seq, HC, HEAD_DIM))
            qs = matmul(hs, w_in, F32, scale=QK_SCALE, col0=0, n=wc)
            ks = matmul(hs, w_in, F32, col0=wc, n=wc)
            vs = matmul(hs, w_in, F32, col0=2 * wc, n=wc)
            att = sb_sample(qs, ks, vs, cache_c_k[i].reshape(nb, past, wc), cache_c_v[i].reshape(nb, past, wc), t=t)
            ys = matmul_res(att, w_out, ys, tm=1024, tn=512)
            outs["sc_k"].append(ks.reshape(nb, t, HC, HEAD_DIM))
            outs["sc_v"].append(vs.reshape(nb, t, HC, HEAD_DIM))
        wg, wu, wd = cast_bf16(w_gate, layer), cast_bf16(w_up, layer), cast_bf16(w_down, layer)
        yp = matmul_res(gateup(rmsnorm(yp, norm_ffn[layer], BF16), wg, wu), wd, yp)
        ys = matmul_res(gateup(rmsnorm(ys, norm_ffn[layer], BF16), wg, wu), wd, ys)

    y_prompt = rmsnorm(yp, norm_final, F32).reshape(1, seq, d)
    y_sample = rmsnorm(ys, norm_final, F32).reshape(nb, t, d)
    st = {name: jnp.stack(v) for name, v in outs.items()}
    return (y_prompt, y_sample, st["pa_k"], st["pa_v"], st["pb_k"], st["pb_v"], st["pc_k"], st["pc_v"],
            st["sa_k"], st["sa_v"], st["sb_k"], st["sb_v"], st["sc_k"], st["sc_v"])
```

```python
import functools

import jax
import jax.numpy as jnp
from jax import lax
from jax.experimental import pallas as pl
from jax.experimental.pallas import tpu as pltpu

HEAD_DIM = 128
CHUNK = 64
HA = 16
KV_A = 4
G_A = HA // KV_A
HB = 16
HC = 16
A_BAND_CHUNKS = 3
B_BAND_CHUNKS = 9
A_REACH = (A_BAND_CHUNKS - 1) * CHUNK
B_REACH = (B_BAND_CHUNKS - 1) * CHUNK
REL_CLIP = 256
N_REL = 2 * REL_CLIP + 1
SB_BLOCK = 128
ROPE_THETA = 10000.0
RMS_EPS = 1e-6
NEG_INF = -1e30
QA_W = HA * HEAD_DIM
KA_W = KV_A * HEAD_DIM
QB_W = HB * HEAD_DIM
AB_SPLITS = (QA_W, QA_W + KA_W, QA_W + 2 * KA_W, QA_W + 2 * KA_W + QB_W, QA_W + 2 * KA_W + 2 * QB_W)
QK_SCALE = HEAD_DIM ** -0.5

SB_QUERY_ROWS = 512
SB_LOOKBACK = 2
SB_SKIP_LOG = 106.0
SUB = 128
BAND_QB = 512
VMEM_LIMIT_MB = 56

F32 = jnp.float32
BF16 = jnp.bfloat16


def _params(semantics):
    return pltpu.CompilerParams(dimension_semantics=semantics, vmem_limit_bytes=VMEM_LIMIT_MB << 20)


def _dot(a, b):
    return jnp.dot(a, b, preferred_element_type=F32)


def _dot_t(a, b):
    return lax.dot_general(a, b, (((1,), (1,)), ((), ())), preferred_element_type=F32)


def _rmsnorm_kernel(x_ref, g_ref, o_ref):
    x = x_ref[...]
    ms = jnp.mean(x * x, axis=-1, keepdims=True)
    o_ref[...] = (x * lax.rsqrt(ms + RMS_EPS) * g_ref[...]).astype(o_ref.dtype)


def rmsnorm(x, g, out_dtype):
    m, d = x.shape
    tm = min(256, m)
    return pl.pallas_call(
        _rmsnorm_kernel,
        grid=(m // tm,),
        in_specs=[pl.BlockSpec((tm, d), lambda i: (i, 0)), pl.BlockSpec((1, d), lambda i: (0, 0))],
        out_specs=pl.BlockSpec((tm, d), lambda i: (i, 0)),
        out_shape=jax.ShapeDtypeStruct((m, d), out_dtype),
        compiler_params=_params(("parallel",)),
        name="rmsnorm",
    )(x, g.reshape(1, d))


def _cast_kernel(x_ref, o_ref):
    o_ref[...] = x_ref[...].astype(o_ref.dtype)


def cast_bf16(w_stack, layer):
    _, k, n = w_stack.shape
    tr = min(256, k)
    assert k % tr == 0
    return pl.pallas_call(
        _cast_kernel,
        grid=(k // tr,),
        in_specs=[pl.BlockSpec((None, tr, n), lambda i: (layer, i, 0))],
        out_specs=pl.BlockSpec((tr, n), lambda i: (i, 0)),
        out_shape=jax.ShapeDtypeStruct((k, n), BF16),
        compiler_params=_params(("parallel",)),
        name="cast_bf16",
    )(w_stack)


def _mm_kernel(a_ref, w_ref, o_ref, *, scale):
    acc = _dot(a_ref[...], w_ref[...])
    if scale is not None:
        acc = acc * scale
    o_ref[...] = acc.astype(o_ref.dtype)


def _mm_heads_kernel(a_ref, w_ref, *rest):
    o32_ref, o16_ref = rest[-2:]
    acc = _dot(a_ref[...], w_ref[...])
    for h in range(o32_ref.shape[1]):
        o32_ref[:, h, :] = acc[:, h * HEAD_DIM:(h + 1) * HEAD_DIM]
    o16_ref[...] = acc.astype(o16_ref.dtype)


def _mm_res_kernel(a_ref, w_ref, r_ref, o_ref):
    o_ref[...] = r_ref[...] + _dot(a_ref[...], w_ref[...])


def _mm2_res_kernel(a1_ref, a2_ref, w_ref, r_ref, o_ref):
    k1 = a1_ref.shape[1]
    acc = _dot(a1_ref[...], w_ref[:k1, :]) + _dot(a2_ref[...], w_ref[k1:, :])
    o_ref[...] = r_ref[...] + acc


def _gateup_kernel(a_ref, wg_ref, wu_ref, o_ref):
    a = a_ref[...]
    g = _dot(a, wg_ref[...])
    u = _dot(a, wu_ref[...])
    o_ref[...] = (g / (1.0 + jnp.exp(-g)) * u).astype(o_ref.dtype)


def _mm_ab_kernel(a_ref, w_ref, cos_ref, sin_ref, o_ref, *, tn):
    col0 = pl.program_id(1) * tn
    acc = _dot(a_ref[...], w_ref[...])
    is_rope = col0 < AB_SPLITS[1]
    is_q = (col0 < AB_SPLITS[0]) | ((col0 >= AB_SPLITS[2]) & (col0 < AB_SPLITS[3]))
    s = jnp.where(is_q, QK_SCALE, 1.0).astype(F32)

    @pl.when(is_rope)
    def _():
        cos = cos_ref[...]
        sin = sin_ref[...]
        for g in range(tn // HEAD_DIM):
            blk = acc[:, g * HEAD_DIM:(g + 1) * HEAD_DIM]
            rot = blk * cos + pltpu.roll(blk, HEAD_DIM // 2, axis=1) * sin
            o_ref[:, g * HEAD_DIM:(g + 1) * HEAD_DIM] = (rot * s).astype(o_ref.dtype)

    @pl.when(jnp.logical_not(is_rope))
    def _():
        o_ref[...] = (acc * s).astype(o_ref.dtype)


def _tiles(m, n, tm, tn):
    tm = min(tm, m)
    tn = min(tn, n)
    assert m % tm == 0 and n % tn == 0, (m, n, tm, tn)
    return tm, tn


def matmul(a, w, out_dtype, *, scale=None, col0=0, n=None, tm=1024, tn=512):
    m, k = a.shape
    n = w.shape[1] - col0 if n is None else n
    tm, tn = _tiles(m, n, tm, tn)
    assert col0 % tn == 0
    c0 = col0 // tn
    return pl.pallas_call(
        functools.partial(_mm_kernel, scale=scale),
        grid=(m // tm, n // tn),
        in_specs=[pl.BlockSpec((tm, k), lambda i, j: (i, 0)), pl.BlockSpec((k, tn), lambda i, j: (0, c0 + j))],
        out_specs=pl.BlockSpec((tm, tn), lambda i, j: (i, j)),
        out_shape=jax.ShapeDtypeStruct((m, n), out_dtype),
        compiler_params=_params(("parallel", "arbitrary")),
        name="matmul",
    )(a, w)


def matmul_heads(a, w, stack, layer, n_layers, *, col0, n, tm=1024, tn=1024):
    m, k = a.shape
    tm, tn = _tiles(m, n, tm, tn)
    assert col0 % tn == 0 and tn % (8 * HEAD_DIM) == 0
    c0 = col0 // tn
    hpt = tn // HEAD_DIM
    in_specs = [pl.BlockSpec((tm, k), lambda i, j: (i, 0)), pl.BlockSpec((k, tn), lambda i, j: (0, c0 + j))]
    args = [a, w]
    if stack is not None:
        in_specs.append(pl.BlockSpec(memory_space=pl.ANY))
        args.append(stack)
    return pl.pallas_call(
        _mm_heads_kernel,
        grid=(m // tm, n // tn),
        in_specs=in_specs,
        out_specs=[pl.BlockSpec((None, tm, hpt, HEAD_DIM), lambda i, j: (layer, i, j, 0)),
                   pl.BlockSpec((tm, tn), lambda i, j: (i, j))],
        out_shape=[jax.ShapeDtypeStruct((n_layers, m, n // HEAD_DIM, HEAD_DIM), F32), jax.ShapeDtypeStruct((m, n), BF16)],
        input_output_aliases={} if stack is None else {2: 0},
        compiler_params=_params(("parallel", "arbitrary")),
        name="matmul_heads",
    )(*args)


def matmul_res(a, w, res, *, tm=512, tn=512):
    m, k = a.shape
    n = w.shape[1]
    tm, tn = _tiles(m, n, tm, tn)
    return pl.pallas_call(
        _mm_res_kernel,
        grid=(m // tm, n // tn),
        in_specs=[pl.BlockSpec((tm, k), lambda i, j: (i, 0)), pl.BlockSpec((k, tn), lambda i, j: (0, j)),
                  pl.BlockSpec((tm, tn), lambda i, j: (i, j))],
        out_specs=pl.BlockSpec((tm, tn), lambda i, j: (i, j)),
        out_shape=jax.ShapeDtypeStruct((m, n), F32),
        compiler_params=_params(("parallel", "arbitrary")),
        name="matmul_res",
    )(a, w, res)


def matmul2_res(a1, a2, w, res, *, tm=1024, tn=512):
    m, k1 = a1.shape
    k2 = a2.shape[1]
    n = w.shape[1]
    tm, tn = _tiles(m, n, tm, tn)
    return pl.pallas_call(
        _mm2_res_kernel,
        grid=(m // tm, n // tn),
        in_specs=[pl.BlockSpec((tm, k1), lambda i, j: (i, 0)), pl.BlockSpec((tm, k2), lambda i, j: (i, 0)),
                  pl.BlockSpec((k1 + k2, tn), lambda i, j: (0, j)), pl.BlockSpec((tm, tn), lambda i, j: (i, j))],
        out_specs=pl.BlockSpec((tm, tn), lambda i, j: (i, j)),
        out_shape=jax.ShapeDtypeStruct((m, n), F32),
        compiler_params=_params(("parallel", "arbitrary")),
        name="matmul2_res",
    )(a1, a2, w, res)


def gateup(a, wg, wu, *, tm=2048, tn=256):
    m, k = a.shape
    n = wg.shape[1]
    tm, tn = _tiles(m, n, tm, tn)
    return pl.pallas_call(
        _gateup_kernel,
        grid=(m // tm, n // tn),
        in_specs=[pl.BlockSpec((tm, k), lambda i, j: (i, 0)), pl.BlockSpec((k, tn), lambda i, j: (0, j)),
                  pl.BlockSpec((k, tn), lambda i, j: (0, j))],
        out_specs=pl.BlockSpec((tm, tn), lambda i, j: (i, j)),
        out_shape=jax.ShapeDtypeStruct((m, n), BF16),
        compiler_params=_params(("parallel", "arbitrary")),
        name="gateup",
    )(a, wg, wu)


def matmul_ab(a, w, cos, sin, out_dtype, *, tm=1024):
    m, k = a.shape
    n = w.shape[1]
    tn = 512
    assert all(s % tn == 0 for s in AB_SPLITS)
    tm, tn = _tiles(m, n, tm, tn)
    return pl.pallas_call(
        functools.partial(_mm_ab_kernel, tn=tn),
        grid=(m // tm, n // tn),
        in_specs=[pl.BlockSpec((tm, k), lambda i, j: (i, 0)), pl.BlockSpec((k, tn), lambda i, j: (0, j)),
                  pl.BlockSpec((tm, HEAD_DIM), lambda i, j: (i, 0)), pl.BlockSpec((tm, HEAD_DIM), lambda i, j: (i, 0))],
        out_specs=pl.BlockSpec((tm, tn), lambda i, j: (i, j)),
        out_shape=jax.ShapeDtypeStruct((m, n), out_dtype),
        compiler_params=_params(("parallel", "arbitrary")),
        name="matmul_ab",
    )(a, w, cos, sin)


def _band_mask(rows, halo, n_band):
    r = lax.broadcasted_iota(jnp.int32, (rows, halo + SUB), 0)
    c = lax.broadcasted_iota(jnp.int32, (rows, halo + SUB), 1)
    rc = (r + halo) >> 6
    cc = c >> 6
    return (cc <= rc) & (cc > rc - n_band)


def _bias_prep_kernel(tab_ref, o_ref):
    h = pl.program_id(0)
    width = B_REACH + SUB
    ulen = width + SUB
    n = lax.broadcasted_iota(jnp.int32, (8, ulen), 1)
    idx = jnp.clip(B_REACH + SUB - 1 - n, -REL_CLIP, REL_CLIP) + REL_CLIP

    def body(m, u):
        return jnp.where(idx == m, tab_ref[h, m], u)

    u = lax.fori_loop(0, N_REL, body, jnp.zeros((8, ulen), F32))
    x = jnp.broadcast_to(u[0:1, :], (SUB, ulen))
    x = pltpu.roll(x, ulen - (SUB - 1), axis=1, stride=1, stride_axis=0)
    o_ref[0] = jnp.where(_band_mask(SUB, B_REACH, B_BAND_CHUNKS), x[:, :width], NEG_INF)


def bias_prep(table):
    return pl.pallas_call(
        _bias_prep_kernel,
        grid=(HB,),
        in_specs=[pl.BlockSpec(memory_space=pltpu.SMEM)],
        out_specs=pl.BlockSpec((1, SUB, B_REACH + SUB), lambda h: (h, 0, 0)),
        out_shape=jax.ShapeDtypeStruct((HB, SUB, B_REACH + SUB), F32),
        compiler_params=_params(("arbitrary",)),
        name="bias_prep",
    )(table)


def _band_prompt_kernel(*refs, group, halo, n_band, has_bias, has_sink):
    q_ref, km_ref, kh_ref, vm_ref, vh_ref = refs[:5]
    rest = list(refs[5:])
    bias_ref = rest.pop(0) if has_bias else None
    sink_ref = rest.pop(0) if has_sink else None
    o_ref = rest.pop(0)
    h = pl.program_id(0)
    i = pl.program_id(1)
    qb = q_ref.shape[0]
    width = halo + SUB
    k = jnp.concatenate([kh_ref[...], km_ref[...]], axis=0)
    v = jnp.concatenate([vh_ref[...], vm_ref[...]], axis=0)
    c = lax.broadcasted_iota(jnp.int32, (SUB, width), 1)
    if has_bias:
        bias = bias_ref[0]
    else:
        bias = jnp.where(_band_mask(SUB, halo, n_band), 0.0, NEG_INF).astype(F32)
    for sb in range(qb // SUB):
        kw = k[sb * SUB:sb * SUB + width]
        vw = v[sb * SUB:sb * SUB + width]
        first_valid = halo - i * qb - sb * SUB
        for g in range(group):
            q = q_ref[sb * SUB:(sb + 1) * SUB, g * HEAD_DIM:(g + 1) * HEAD_DIM]
            s = _dot_t(q, kw) + bias
            s = jnp.where(c >= first_valid, s, NEG_INF)
            m = jnp.max(s, axis=-1, keepdims=True)
            if has_sink:
                sk = sink_ref[h * group + g]
                m = jnp.maximum(m, sk)
            p = jnp.exp(s - m)
            l = jnp.sum(p, axis=-1, keepdims=True)
            if has_sink:
                l = l + jnp.exp(sk - m)
            o = _dot(p.astype(BF16), vw) / l
            o_ref[sb * SUB:(sb + 1) * SUB, g * HEAD_DIM:(g + 1) * HEAD_DIM] = o.astype(o_ref.dtype)


def band_prompt(proj, *, q_col, k_col, v_col, n_kv, group, halo, n_band, bias=None, sink=None):
    s = proj.shape[0]
    qb = BAND_QB
    assert s % qb == 0 and qb % halo == 0 and halo % CHUNK == 0
    qw = group * HEAD_DIM
    hpb = qb // halo
    qc, kc, vc = q_col // qw, k_col // HEAD_DIM, v_col // HEAD_DIM
    main = lambda c0: pl.BlockSpec((qb, HEAD_DIM), lambda h, i: (i, c0 + h))
    halo_spec = lambda c0: pl.BlockSpec((halo, HEAD_DIM), lambda h, i: (jnp.maximum(i * hpb - 1, 0), c0 + h))
    in_specs = [pl.BlockSpec((qb, qw), lambda h, i: (i, qc + h)), main(kc), halo_spec(kc), main(vc), halo_spec(vc)]
    args = [proj, proj, proj, proj, proj]
    if bias is not None:
        in_specs.append(pl.BlockSpec((1, SUB, halo + SUB), lambda h, i: (h, 0, 0)))
        args.append(bias)
    if sink is not None:
        in_specs.append(pl.BlockSpec(memory_space=pltpu.SMEM))
        args.append(sink)
    return pl.pallas_call(
        functools.partial(_band_prompt_kernel, group=group, halo=halo, n_band=n_band,
                          has_bias=bias is not None, has_sink=sink is not None),
        grid=(n_kv, s // qb),
        in_specs=in_specs,
        out_specs=pl.BlockSpec((qb, qw), lambda h, i: (i, h)),
        out_shape=jax.ShapeDtypeStruct((s, n_kv * qw), BF16),
        compiler_params=_params(("parallel", "arbitrary")),
        name="band_prompt",
    )(*args)


def _band_sample_kernel(*refs, group, n_kv, has_bias, has_sink):
    q_ref, kn_ref, vn_ref, kc_ref, vc_ref = refs[:5]
    rest = list(refs[5:])
    bias_ref = rest.pop(0) if has_bias else None
    sink_ref = rest.pop(0) if has_sink else None
    o_ref, ko_ref, vo_ref = rest[-3:]
    t = q_ref.shape[0]
    lc = kc_ref.shape[0]
    ko_ref[:lc - t] = kc_ref[t:]
    vo_ref[:lc - t] = vc_ref[t:]
    if has_sink:
        row = lax.broadcasted_iota(jnp.int32, (group * t, 1), 0)
    for h in range(n_kv):
        cols = slice(h * HEAD_DIM, (h + 1) * HEAD_DIM)
        kn = kn_ref[:, cols]
        vn = vn_ref[:, cols]
        ko_ref[lc - t:, h, :] = kn
        vo_ref[lc - t:, h, :] = vn
        kc = kc_ref[:, h, :].astype(BF16)
        vc = vc_ref[:, h, :].astype(BF16)
        heads = [h * group + g for g in range(group)]
        q = jnp.concatenate([q_ref[:, j * HEAD_DIM:(j + 1) * HEAD_DIM] for j in heads], axis=0).astype(BF16)
        sc = _dot_t(q, kc)
        sn = _dot_t(q, kn.astype(BF16))
        if has_bias:
            sc = sc + bias_ref[h, :t, :lc]
            sn = sn + bias_ref[h, :t, lc:lc + t]
        m = jnp.maximum(jnp.max(sc, axis=-1, keepdims=True), jnp.max(sn, axis=-1, keepdims=True))
        if has_sink:
            sk = jnp.zeros((group * t, 1), F32)
            for g, j in enumerate(heads):
                sk = jnp.where((row >= g * t) & (row < (g + 1) * t), sink_ref[j], sk)
            m = jnp.maximum(m, sk)
        pc = jnp.exp(sc - m)
        pn = jnp.exp(sn - m)
        l = jnp.sum(pc, axis=-1, keepdims=True) + jnp.sum(pn, axis=-1, keepdims=True)
        if has_sink:
            l = l + jnp.exp(sk - m)
        o = (_dot(pc.astype(BF16), vc) + _dot(pn.astype(BF16), vn.astype(BF16))) / l
        for g, j in enumerate(heads):
            o_ref[:, j * HEAD_DIM:(j + 1) * HEAD_DIM] = o[g * t:(g + 1) * t].astype(o_ref.dtype)


def band_sample(q, k_new, v_new, cache_k, cache_v, stack_k, stack_v, layer, *, t, group, bias=None, sink=None):
    _, nb, lc, n_kv, _ = cache_k.shape
    rows = lambda width: pl.BlockSpec((t, width), lambda b: (b, 0))
    cache_spec = pl.BlockSpec((None, None, lc, n_kv, HEAD_DIM), lambda b: (layer, b, 0, 0, 0))
    in_specs = [rows(q.shape[1]), rows(k_new.shape[1]), rows(v_new.shape[1]), cache_spec, cache_spec]
    args = [q, k_new, v_new, cache_k, cache_v]
    if bias is not None:
        assert lc == B_REACH and t <= CHUNK
        in_specs.append(pl.BlockSpec(bias.shape, lambda b: (0, 0, 0)))
        args.append(bias)
    if sink is not None:
        in_specs.append(pl.BlockSpec(memory_space=pltpu.SMEM))
        args.append(sink)
    aliases = {}
    if stack_k is not None:
        aliases = {len(args): 1, len(args) + 1: 2}
        in_specs += [pl.BlockSpec(memory_space=pl.ANY)] * 2
        args += [stack_k, stack_v]
    return pl.pallas_call(
        functools.partial(_band_sample_kernel, group=group, n_kv=n_kv, has_bias=bias is not None,
                          has_sink=sink is not None),
        grid=(nb,),
        in_specs=in_specs,
        out_specs=[rows(q.shape[1]), cache_spec, cache_spec],
        out_shape=[jax.ShapeDtypeStruct(q.shape, BF16),
                   jax.ShapeDtypeStruct(cache_k.shape, F32), jax.ShapeDtypeStruct(cache_v.shape, F32)],
        input_output_aliases=aliases,
        compiler_params=_params(("parallel",)),
        name="band_sample",
    )(*args)


def _tri2():
    j = lax.broadcasted_iota(jnp.int32, (SB_BLOCK, 2 * SB_BLOCK), 0)
    s = lax.broadcasted_iota(jnp.int32, (SB_BLOCK, 2 * SB_BLOCK), 1)
    return jnp.where((s >= SB_BLOCK) | (j > s), 1.0, 0.0).astype(BF16)


def _softplus(z):
    neg_abs = lax.bitcast_convert_type(lax.bitcast_convert_type(z, jnp.int32) | jnp.int32(-2 ** 31), F32)
    return jnp.maximum(z, 0.0) + jnp.log(1.0 + jnp.exp(neg_abs))


def _sb_tile(z, tri2, carry):
    nls = _softplus(z)
    la = _dot(nls.astype(BF16), tri2)
    logw = z - nls - la[:, :SB_BLOCK]
    if carry is None:
        return logw, la[:, SB_BLOCK:]
    return logw - carry, carry + la[:, SB_BLOCK:]


def _causal_tile_mask(rows):
    r = lax.broadcasted_iota(jnp.int32, (rows, SB_BLOCK), 0)
    c = lax.broadcasted_iota(jnp.int32, (rows, SB_BLOCK), 1)
    return c < r


def _sb_block(ref, j):
    return ref[pl.ds(pl.multiple_of(j * SB_BLOCK, SB_BLOCK), SB_BLOCK), :].astype(BF16)


def _sb_older_blocks(q, k_ref, v_ref, tri2, first, acc, carry):
    def cond(state):
        j, _, carry = state
        return (j >= 0) & (jnp.min(carry) < SB_SKIP_LOG)

    def body(state):
        j, acc, carry = state
        logw, carry = _sb_tile(_dot_t(q, _sb_block(k_ref, j)), tri2, carry)
        acc = acc + _dot(jnp.exp(logw).astype(BF16), _sb_block(v_ref, j))
        return j - 1, acc, carry

    _, acc, _ = lax.while_loop(cond, body, (first, acc, carry))
    return acc


def _sb_prompt_kernel(q_ref, k_ref, v_ref, o_ref):
    tq = q_ref.shape[0]
    nsub = tq // SB_BLOCK
    nwin = nsub + SB_LOOKBACK
    pid = pl.program_id(1)
    base = pid * nsub
    is_first = pid == 0
    start = pl.multiple_of(jnp.maximum(base - SB_LOOKBACK, 0) * SB_BLOCK, SB_BLOCK)
    q = q_ref[...]
    kw = k_ref[pl.ds(start, nwin * SB_BLOCK), :]
    vw = v_ref[pl.ds(start, nwin * SB_BLOCK), :]
    kw = jnp.where(is_first, jnp.roll(kw, SB_LOOKBACK * SB_BLOCK, axis=0), kw)
    vw = jnp.where(is_first, jnp.roll(vw, SB_LOOKBACK * SB_BLOCK, axis=0), vw)
    tri2 = _tri2()
    z = _dot_t(q, kw)
    diag = _causal_tile_mask(SB_BLOCK)
    accs, carries = [], []
    for a in range(nsub):
        rows = slice(a * SB_BLOCK, (a + 1) * SB_BLOCK)
        carry = None
        pieces = []
        for b in range(a + SB_LOOKBACK, -1, -1):
            zt = z[rows, b * SB_BLOCK:(b + 1) * SB_BLOCK]
            if b == a + SB_LOOKBACK:
                zt = jnp.where(diag, zt, NEG_INF)
            if b < SB_LOOKBACK:
                zt = jnp.where(is_first, NEG_INF, zt)
            logw, carry = _sb_tile(zt, tri2, carry)
            pieces.insert(0, logw)
        w = jnp.exp(jnp.concatenate(pieces, axis=1)).astype(BF16)
        accs.append(_dot(w, vw[:(a + SB_LOOKBACK + 1) * SB_BLOCK]))
        carries.append(carry)
    acc = jnp.concatenate(accs, axis=0)
    carry = jnp.concatenate(carries, axis=0)
    acc = _sb_older_blocks(q, k_ref, v_ref, tri2, base - SB_LOOKBACK - 1, acc, carry)
    o_ref[...] = acc.astype(o_ref.dtype)


def sb_prompt(q, k, v):
    s = q.shape[0]
    tq = min(SB_QUERY_ROWS, s)
    assert s % tq == 0 and tq % SB_BLOCK == 0 and s >= tq + SB_LOOKBACK * SB_BLOCK
    head_all = pl.BlockSpec((s, HEAD_DIM), lambda h, i: (0, h))
    return pl.pallas_call(
        _sb_prompt_kernel,
        grid=(HC, s // tq),
        in_specs=[pl.BlockSpec((tq, HEAD_DIM), lambda h, i: (i, h)), head_all, head_all],
        out_specs=pl.BlockSpec((tq, HEAD_DIM), lambda h, i: (i, h)),
        out_shape=jax.ShapeDtypeStruct(q.shape, BF16),
        compiler_params=_params(("parallel", "arbitrary")),
        name="sb_prompt",
    )(q, k, v)


def _sb_sample_kernel(q_ref, kn_ref, vn_ref, kc_ref, vc_ref, o_ref, acc_ref, carry_ref):
    s = pl.program_id(1)
    t = q_ref.shape[0]
    n_heads = acc_ref.shape[0]
    tri2 = _tri2()

    def head_q(h):
        return q_ref[:, h * HEAD_DIM:(h + 1) * HEAD_DIM].astype(BF16)

    @pl.when(s == 0)
    def _():
        zeros = jnp.zeros((SB_BLOCK - t, HEAD_DIM), BF16)
        mask = _causal_tile_mask(t)
        for h in range(n_heads):
            cols = slice(h * HEAD_DIM, (h + 1) * HEAD_DIM)
            kb = jnp.concatenate([kn_ref[:, cols].astype(BF16), zeros], axis=0)
            vb = jnp.concatenate([vn_ref[:, cols].astype(BF16), zeros], axis=0)
            logw, carry = _sb_tile(jnp.where(mask, _dot_t(head_q(h), kb), NEG_INF), tri2, None)
            acc_ref[h] = _dot(jnp.exp(logw).astype(BF16), vb)
            carry_ref[h] = carry

    @pl.when(jnp.min(carry_ref[...]) < SB_SKIP_LOG)
    def _():
        for h in range(n_heads):
            kb = kc_ref[:, h, :].astype(BF16)
            vb = vc_ref[:, h, :].astype(BF16)
            logw, carry = _sb_tile(_dot_t(head_q(h), kb), tri2, carry_ref[h])
            acc_ref[h] += _dot(jnp.exp(logw).astype(BF16), vb)
            carry_ref[h] = carry

    @pl.when(s == pl.num_programs(1) - 1)
    def _():
        for h in range(n_heads):
            o_ref[:, h * HEAD_DIM:(h + 1) * HEAD_DIM] = acc_ref[h].astype(o_ref.dtype)


def sb_sample(q, k_new, v_new, cache_k, cache_v, layer, *, t):
    _, nb, past, n_heads, _ = cache_k.shape
    assert past % SB_BLOCK == 0 and t <= SB_BLOCK
    nblk = past // SB_BLOCK
    rows = pl.BlockSpec((t, n_heads * HEAD_DIM), lambda b, s: (b, 0))
    cache_spec = pl.BlockSpec((None, None, SB_BLOCK, n_heads, HEAD_DIM), lambda b, s: (layer, b, nblk - 1 - s, 0, 0))
    return pl.pallas_call(
        _sb_sample_kernel,
        grid=(nb, nblk),
        in_specs=[rows, rows, rows, cache_spec, cache_spec],
        out_specs=rows,
        out_shape=jax.ShapeDtypeStruct(q.shape, BF16),
        scratch_shapes=[pltpu.VMEM((n_heads, t, HEAD_DIM), F32), pltpu.VMEM((n_heads, t, HEAD_DIM), F32)],
        compiler_params=_params(("parallel", "arbitrary")),
        name="sb_sample",
    )(q, k_new, v_new, cache_k, cache_v)


def _rope_table_kernel(inv_ref, cos_ref, sin_ref, *, pos0):
    tm = cos_ref.shape[0]
    pos = lax.broadcasted_iota(jnp.int32, (tm, HEAD_DIM), 0) + (pos0 + pl.program_id(0) * tm)
    lane = lax.broadcasted_iota(jnp.int32, (tm, HEAD_DIM), 1)
    ang = pos.astype(F32) * inv_ref[...]
    sin = jnp.sin(ang)
    cos_ref[...] = jnp.cos(ang)
    sin_ref[...] = jnp.where(lane < HEAD_DIM // 2, -sin, sin)


def _rope_tables(n, pos0):
    half = HEAD_DIM // 2
    inv = ROPE_THETA ** (-jnp.arange(half, dtype=F32) / half)
    inv = jnp.concatenate([inv, inv]).reshape(1, HEAD_DIM)
    tm = min(1024, n)
    assert n % tm == 0
    return pl.pallas_call(
        functools.partial(_rope_table_kernel, pos0=pos0),
        grid=(n // tm,),
        in_specs=[pl.BlockSpec((1, HEAD_DIM), lambda i: (0, 0))],
        out_specs=[pl.BlockSpec((tm, HEAD_DIM), lambda i: (i, 0))] * 2,
        out_shape=[jax.ShapeDtypeStruct((n, HEAD_DIM), F32)] * 2,
        compiler_params=_params(("parallel",)),
        name="rope_tables",
    )(inv)


def kernel(x_prompt, x_sample, cache_a_k, cache_a_v, cache_b_k, cache_b_v, cache_c_k, cache_c_v, norm_mix, w_in_ab, sink_a, rel_bias_b, w_out_ab, w_in_c, w_out_c, norm_ffn, w_gate, w_up, w_down, norm_final):
    bp, seq, d = x_prompt.shape
    nb, t, _ = x_sample.shape
    depth = norm_mix.shape[0]
    past = cache_c_k.shape[2]
    assert bp == 1 and seq >= B_REACH
    keep_a, keep_b = min(A_REACH, seq), min(B_REACH, seq)
    wc = HC * HEAD_DIM

    yp = x_prompt.reshape(seq, d)
    ys = x_sample.reshape(nb * t, d)
    cos_p, sin_p = _rope_tables(seq, 0)
    cos_s, sin_s = _rope_tables(t, past)
    cos_s, sin_s = jnp.tile(cos_s, (nb, 1)), jnp.tile(sin_s, (nb, 1))

    outs = {name: [] for name in ("pa_k", "pa_v", "pb_k", "pb_v", "sc_k", "sc_v")}
    n_c = cache_c_k.shape[0]
    sa_k = sa_v = sb_k = sb_v = pc_k = pc_v = None
    for layer in range(depth):
        i = layer // 2
        hp = rmsnorm(yp, norm_mix[layer], BF16)
        hs = rmsnorm(ys, norm_mix[layer], BF16)
        if layer % 2 == 0:
            w_in = cast_bf16(w_in_ab, i)
            w_out = cast_bf16(w_out_ab, i)
            bias = bias_prep(rel_bias_b[i])
            proj = matmul_ab(hp, w_in, cos_p, sin_p, BF16)
            oa = band_prompt(proj, q_col=0, k_col=AB_SPLITS[0], v_col=AB_SPLITS[1], n_kv=KV_A, group=G_A,
                             halo=A_REACH, n_band=A_BAND_CHUNKS, sink=sink_a[i])
            ob = band_prompt(proj, q_col=AB_SPLITS[2], k_col=AB_SPLITS[3], v_col=AB_SPLITS[4], n_kv=HB, group=1,
                             halo=B_REACH, n_band=B_BAND_CHUNKS, bias=bias)
            yp = matmul2_res(oa, ob, w_out, yp)
            tail = matmul_ab(hp[seq - keep_b:], w_in, cos_p[seq - keep_b:], sin_p[seq - keep_b:], F32)
            outs["pa_k"].append(tail[keep_b - keep_a:, AB_SPLITS[0]:AB_SPLITS[1]].reshape(1, keep_a, KV_A, HEAD_DIM))
            outs["pa_v"].append(tail[keep_b - keep_a:, AB_SPLITS[1]:AB_SPLITS[2]].reshape(1, keep_a, KV_A, HEAD_DIM))
            outs["pb_k"].append(tail[:, AB_SPLITS[3]:AB_SPLITS[4]].reshape(1, keep_b, HB, HEAD_DIM))
            outs["pb_v"].append(tail[:, AB_SPLITS[4]:].reshape(1, keep_b, HB, HEAD_DIM))
            projs = matmul_ab(hs, w_in, cos_s, sin_s, F32)
            qa, ka, va, qb, kb, vb = jnp.split(projs, AB_SPLITS, axis=1)
            oa, sa_k, sa_v = band_sample(qa, ka, va, cache_a_k, cache_a_v, sa_k, sa_v, i, t=t, group=G_A,
                                         sink=sink_a[i])
            ob, sb_k, sb_v = band_sample(qb, kb, vb, cache_b_k, cache_b_v, sb_k, sb_v, i, t=t, group=1, bias=bias)
            ys = matmul2_res(oa, ob, w_out, ys)
        else:
            w_in = cast_bf16(w_in_c, i)
            w_out = cast_bf16(w_out_c, i)
            q = matmul(hp, w_in, BF16, scale=QK_SCALE, col0=0, n=wc)
            pc_k, k16 = matmul_heads(hp, w_in, pc_k, i, n_c, col0=wc, n=wc)
            pc_v, v16 = matmul_heads(hp, w_in, pc_v, i, n_c, col0=2 * wc, n=wc)
            att = sb_prompt(q, k16, v16)
            yp = matmul_res(att, w_out, yp, tm=1024, tn=512)
            qs = matmul(hs, w_in, F32, scale=QK_SCALE, col0=0, n=wc)
            ks = matmul(hs, w_in, F32, col0=wc, n=wc)
            vs = matmul(hs, w_in, F32, col0=2 * wc, n=wc)
            att = sb_sample(qs, ks, vs, cache_c_k, cache_c_v, i, t=t)
            ys = matmul_res(att, w_out, ys, tm=1024, tn=512)
            outs["sc_k"].append(ks.reshape(nb, t, HC, HEAD_DIM))
            outs["sc_v"].append(vs.reshape(nb, t, HC, HEAD_DIM))
        wg, wu, wd = cast_bf16(w_gate, layer), cast_bf16(w_up, layer), cast_bf16(w_down, layer)
        yp = matmul_res(gateup(rmsnorm(yp, norm_ffn[layer], BF16), wg, wu), wd, yp)
        ys = matmul_res(gateup(rmsnorm(ys, norm_ffn[layer], BF16), wg, wu), wd, ys)

    y_prompt = rmsnorm(yp, norm_final, F32).reshape(1, seq, d)
    y_sample = rmsnorm(ys, norm_final, F32).reshape(nb, t, d)
    st = {name: jnp.stack(v) for name, v in outs.items()}
    pc_k = pc_k.reshape(n_c, 1, seq, HC, HEAD_DIM)
    pc_v = pc_v.reshape(n_c, 1, seq, HC, HEAD_DIM)
    return (y_prompt, y_sample, st["pa_k"], st["pa_v"], st["pb_k"], st["pb_v"], pc_k, pc_v,
            sa_k, sa_v, sb_k, sb_v, st["sc_k"], st["sc_v"])
```

```python
import functools

import jax
import jax.numpy as jnp
from jax import lax
from jax.experimental import pallas as pl
from jax.experimental.pallas import tpu as pltpu

HEAD_DIM = 128
CHUNK = 64
HA = 16
KV_A = 4
G_A = HA // KV_A
HB = 16
HC = 16
A_BAND_CHUNKS = 3
B_BAND_CHUNKS = 9
A_REACH = (A_BAND_CHUNKS - 1) * CHUNK
B_REACH = (B_BAND_CHUNKS - 1) * CHUNK
REL_CLIP = 256
N_REL = 2 * REL_CLIP + 1
SB_BLOCK = 128
ROPE_THETA = 10000.0
RMS_EPS = 1e-6
NEG_INF = -1e30
QA_W = HA * HEAD_DIM
KA_W = KV_A * HEAD_DIM
QB_W = HB * HEAD_DIM
AB_SPLITS = (QA_W, QA_W + KA_W, QA_W + 2 * KA_W, QA_W + 2 * KA_W + QB_W, QA_W + 2 * KA_W + 2 * QB_W)
QK_SCALE = HEAD_DIM ** -0.5

SB_QUERY_ROWS = 512
SB_LOOKBACK = 2
SB_SKIP_LOG = 106.0
SUB = 128
BAND_QB = 512
VMEM_LIMIT_MB = 56

F32 = jnp.float32
BF16 = jnp.bfloat16


def _params(semantics):
    return pltpu.CompilerParams(dimension_semantics=semantics, vmem_limit_bytes=VMEM_LIMIT_MB << 20)


def _dot(a, b):
    return jnp.dot(a, b, preferred_element_type=F32)


def _dot_t(a, b):
    return lax.dot_general(a, b, (((1,), (1,)), ((), ())), preferred_element_type=F32)


def _rmsnorm_kernel(x_ref, g_ref, o_ref):
    x = x_ref[...]
    ms = jnp.mean(x * x, axis=-1, keepdims=True)
    o_ref[...] = (x * lax.rsqrt(ms + RMS_EPS) * g_ref[...]).astype(o_ref.dtype)


def rmsnorm(x, g, out_dtype):
    m, d = x.shape
    tm = min(256, m)
    return pl.pallas_call(
        _rmsnorm_kernel,
        grid=(m // tm,),
        in_specs=[pl.BlockSpec((tm, d), lambda i: (i, 0)), pl.BlockSpec((1, d), lambda i: (0, 0))],
        out_specs=pl.BlockSpec((tm, d), lambda i: (i, 0)),
        out_shape=jax.ShapeDtypeStruct((m, d), out_dtype),
        compiler_params=_params(("parallel",)),
        name="rmsnorm",
    )(x, g.reshape(1, d))


def _lane_partial_ssq(y):
    sq = y * y
    part = sq[:, :HEAD_DIM]
    for c in range(1, y.shape[1] // HEAD_DIM):
        part = part + sq[:, c * HEAD_DIM:(c + 1) * HEAD_DIM]
    return part


def _row_rstd(ssq_ref, d):
    return lax.rsqrt(jnp.sum(ssq_ref[...], axis=-1, keepdims=True) / d + RMS_EPS)


def _norm_inputs_kernel(x_ref, o16_ref, ssq_ref):
    x = x_ref[...]
    o16_ref[...] = x.astype(o16_ref.dtype)
    ssq_ref[...] = _lane_partial_ssq(x)


def norm_inputs(x):
    m, d = x.shape
    tm = min(256, m)
    return pl.pallas_call(
        _norm_inputs_kernel,
        grid=(m // tm,),
        in_specs=[pl.BlockSpec((tm, d), lambda i: (i, 0))],
        out_specs=[pl.BlockSpec((tm, d), lambda i: (i, 0)), pl.BlockSpec((tm, HEAD_DIM), lambda i: (i, 0))],
        out_shape=[jax.ShapeDtypeStruct((m, d), BF16), jax.ShapeDtypeStruct((m, HEAD_DIM), F32)],
        compiler_params=_params(("parallel",)),
        name="norm_inputs",
    )(x)


def _cast_kernel(x_ref, *rest):
    o_ref = rest[-1]
    x = x_ref[...]
    if len(rest) == 2:
        x = x * rest[0][...]
    o_ref[...] = x.astype(o_ref.dtype)


def cast_bf16(w_stack, layer, gain=None):
    _, k, n = w_stack.shape
    tr = min(256, k)
    assert k % tr == 0
    in_specs = [pl.BlockSpec((None, tr, n), lambda i: (layer, i, 0))]
    args = [w_stack]
    if gain is not None:
        in_specs.append(pl.BlockSpec((tr, 1), lambda i: (i, 0)))
        args.append(gain.reshape(k, 1))
    return pl.pallas_call(
        _cast_kernel,
        grid=(k // tr,),
        in_specs=in_specs,
        out_specs=pl.BlockSpec((tr, n), lambda i: (i, 0)),
        out_shape=jax.ShapeDtypeStruct((k, n), BF16),
        compiler_params=_params(("parallel",)),
        name="cast_bf16",
    )(*args)


def _mm_kernel(a_ref, ssq_ref, w_ref, o_ref, *, scale):
    acc = _dot(a_ref[...], w_ref[...]) * (_row_rstd(ssq_ref, a_ref.shape[1]) * scale)
    o_ref[...] = acc.astype(o_ref.dtype)


def _mm_heads_kernel(a_ref, ssq_ref, w_ref, *rest):
    o32_ref, o16_ref = rest[-2:]
    acc = _dot(a_ref[...], w_ref[...]) * _row_rstd(ssq_ref, a_ref.shape[1])
    for h in range(o32_ref.shape[1]):
        o32_ref[:, h, :] = acc[:, h * HEAD_DIM:(h + 1) * HEAD_DIM]
    o16_ref[...] = acc.astype(o16_ref.dtype)


def _emit_residual(y, o_ref, o16_ref, ssq_ref):
    o_ref[...] = y
    o16_ref[...] = y.astype(o16_ref.dtype)
    part = _lane_partial_ssq(y)
    first = pl.program_id(1) == 0

    @pl.when(first)
    def _():
        ssq_ref[...] = part

    @pl.when(jnp.logical_not(first))
    def _():
        ssq_ref[...] += part


def _mm_res_kernel(a_ref, w_ref, r_ref, o_ref, o16_ref, ssq_ref):
    _emit_residual(r_ref[...] + _dot(a_ref[...], w_ref[...]), o_ref, o16_ref, ssq_ref)


def _mm2_res_kernel(a1_ref, a2_ref, w_ref, r_ref, o_ref, o16_ref, ssq_ref):
    k1 = a1_ref.shape[1]
    acc = _dot(a1_ref[...], w_ref[:k1, :]) + _dot(a2_ref[...], w_ref[k1:, :])
    _emit_residual(r_ref[...] + acc, o_ref, o16_ref, ssq_ref)


def _gateup_kernel(a_ref, ssq_ref, wg_ref, wu_ref, o_ref):
    a = a_ref[...]
    rstd = _row_rstd(ssq_ref, a_ref.shape[1])
    g = _dot(a, wg_ref[...]) * rstd
    u = _dot(a, wu_ref[...]) * rstd
    o_ref[...] = (g / (1.0 + jnp.exp(-g)) * u).astype(o_ref.dtype)


def _mm_ab_kernel(a_ref, ssq_ref, w_ref, cos_ref, sin_ref, o_ref, *, tn):
    col0 = pl.program_id(1) * tn
    acc = _dot(a_ref[...], w_ref[...]) * _row_rstd(ssq_ref, a_ref.shape[1])
    is_rope = col0 < AB_SPLITS[1]
    is_q = (col0 < AB_SPLITS[0]) | ((col0 >= AB_SPLITS[2]) & (col0 < AB_SPLITS[3]))
    s = jnp.where(is_q, QK_SCALE, 1.0).astype(F32)

    @pl.when(is_rope)
    def _():
        cos = cos_ref[...]
        sin = sin_ref[...]
        for g in range(tn // HEAD_DIM):
            blk = acc[:, g * HEAD_DIM:(g + 1) * HEAD_DIM]
            rot = blk * cos + pltpu.roll(blk, HEAD_DIM // 2, axis=1) * sin
            o_ref[:, g * HEAD_DIM:(g + 1) * HEAD_DIM] = (rot * s).astype(o_ref.dtype)

    @pl.when(jnp.logical_not(is_rope))
    def _():
        o_ref[...] = (acc * s).astype(o_ref.dtype)


def _tiles(m, n, tm, tn):
    tm = min(tm, m)
    tn = min(tn, n)
    assert m % tm == 0 and n % tn == 0, (m, n, tm, tn)
    return tm, tn


def _ssq_spec(tm):
    return pl.BlockSpec((tm, HEAD_DIM), lambda i, j: (i, 0))


def _residual_out_specs(tm, tn):
    return [pl.BlockSpec((tm, tn), lambda i, j: (i, j)), pl.BlockSpec((tm, tn), lambda i, j: (i, j)), _ssq_spec(tm)]


def _residual_out_shapes(m, n):
    return [jax.ShapeDtypeStruct((m, n), F32), jax.ShapeDtypeStruct((m, n), BF16),
            jax.ShapeDtypeStruct((m, HEAD_DIM), F32)]


def matmul(a, ssq, w, out_dtype, *, scale=1.0, col0=0, n=None, tm=1024, tn=512):
    m, k = a.shape
    n = w.shape[1] - col0 if n is None else n
    tm, tn = _tiles(m, n, tm, tn)
    assert col0 % tn == 0
    c0 = col0 // tn
    return pl.pallas_call(
        functools.partial(_mm_kernel, scale=scale),
        grid=(m // tm, n // tn),
        in_specs=[pl.BlockSpec((tm, k), lambda i, j: (i, 0)), _ssq_spec(tm),
                  pl.BlockSpec((k, tn), lambda i, j: (0, c0 + j))],
        out_specs=pl.BlockSpec((tm, tn), lambda i, j: (i, j)),
        out_shape=jax.ShapeDtypeStruct((m, n), out_dtype),
        compiler_params=_params(("parallel", "arbitrary")),
        name="matmul",
    )(a, ssq, w)


def matmul_heads(a, ssq, w, stack, layer, n_layers, *, col0, n, tm=1024, tn=1024):
    m, k = a.shape
    tm, tn = _tiles(m, n, tm, tn)
    assert col0 % tn == 0 and tn % (8 * HEAD_DIM) == 0
    c0 = col0 // tn
    hpt = tn // HEAD_DIM
    in_specs = [pl.BlockSpec((tm, k), lambda i, j: (i, 0)), _ssq_spec(tm),
                pl.BlockSpec((k, tn), lambda i, j: (0, c0 + j))]
    args = [a, ssq, w]
    if stack is not None:
        in_specs.append(pl.BlockSpec(memory_space=pl.ANY))
        args.append(stack)
    return pl.pallas_call(
        _mm_heads_kernel,
        grid=(m // tm, n // tn),
        in_specs=in_specs,
        out_specs=[pl.BlockSpec((None, tm, hpt, HEAD_DIM), lambda i, j: (layer, i, j, 0)),
                   pl.BlockSpec((tm, tn), lambda i, j: (i, j))],
        out_shape=[jax.ShapeDtypeStruct((n_layers, m, n // HEAD_DIM, HEAD_DIM), F32), jax.ShapeDtypeStruct((m, n), BF16)],
        input_output_aliases={} if stack is None else {3: 0},
        compiler_params=_params(("parallel", "arbitrary")),
        name="matmul_heads",
    )(*args)


def matmul_res(a, w, res, *, tm=512, tn=512):
    m, k = a.shape
    n = w.shape[1]
    tm, tn = _tiles(m, n, tm, tn)
    return pl.pallas_call(
        _mm_res_kernel,
        grid=(m // tm, n // tn),
        in_specs=[pl.BlockSpec((tm, k), lambda i, j: (i, 0)), pl.BlockSpec((k, tn), lambda i, j: (0, j)),
                  pl.BlockSpec((tm, tn), lambda i, j: (i, j))],
        out_specs=_residual_out_specs(tm, tn),
        out_shape=_residual_out_shapes(m, n),
        compiler_params=_params(("parallel", "arbitrary")),
        name="matmul_res",
    )(a, w, res)


def matmul2_res(a1, a2, w, res, *, tm=1024, tn=512):
    m, k1 = a1.shape
    k2 = a2.shape[1]
    n = w.shape[1]
    tm, tn = _tiles(m, n, tm, tn)
    return pl.pallas_call(
        _mm2_res_kernel,
        grid=(m // tm, n // tn),
        in_specs=[pl.BlockSpec((tm, k1), lambda i, j: (i, 0)), pl.BlockSpec((tm, k2), lambda i, j: (i, 0)),
                  pl.BlockSpec((k1 + k2, tn), lambda i, j: (0, j)), pl.BlockSpec((tm, tn), lambda i, j: (i, j))],
        out_specs=_residual_out_specs(tm, tn),
        out_shape=_residual_out_shapes(m, n),
        compiler_params=_params(("parallel", "arbitrary")),
        name="matmul2_res",
    )(a1, a2, w, res)


def gateup(a, ssq, wg, wu, *, tm=2048, tn=256):
    m, k = a.shape
    n = wg.shape[1]
    tm, tn = _tiles(m, n, tm, tn)
    return pl.pallas_call(
        _gateup_kernel,
        grid=(m // tm, n // tn),
        in_specs=[pl.BlockSpec((tm, k), lambda i, j: (i, 0)), _ssq_spec(tm),
                  pl.BlockSpec((k, tn), lambda i, j: (0, j)), pl.BlockSpec((k, tn), lambda i, j: (0, j))],
        out_specs=pl.BlockSpec((tm, tn), lambda i, j: (i, j)),
        out_shape=jax.ShapeDtypeStruct((m, n), BF16),
        compiler_params=_params(("parallel", "arbitrary")),
        name="gateup",
    )(a, ssq, wg, wu)


def matmul_ab(a, ssq, w, cos, sin, out_dtype, *, tm=2048):
    m, k = a.shape
    n = w.shape[1]
    tn = 512
    assert all(s % tn == 0 for s in AB_SPLITS)
    tm, tn = _tiles(m, n, tm, tn)
    return pl.pallas_call(
        functools.partial(_mm_ab_kernel, tn=tn),
        grid=(m // tm, n // tn),
        in_specs=[pl.BlockSpec((tm, k), lambda i, j: (i, 0)), _ssq_spec(tm), pl.BlockSpec((k, tn), lambda i, j: (0, j)),
                  pl.BlockSpec((tm, HEAD_DIM), lambda i, j: (i, 0)), pl.BlockSpec((tm, HEAD_DIM), lambda i, j: (i, 0))],
        out_specs=pl.BlockSpec((tm, tn), lambda i, j: (i, j)),
        out_shape=jax.ShapeDtypeStruct((m, n), out_dtype),
        compiler_params=_params(("parallel", "arbitrary")),
        name="matmul_ab",
    )(a, ssq, w, cos, sin)


def _band_mask(rows, halo, n_band):
    r = lax.broadcasted_iota(jnp.int32, (rows, halo + SUB), 0)
    c = lax.broadcasted_iota(jnp.int32, (rows, halo + SUB), 1)
    rc = (r + halo) >> 6
    cc = c >> 6
    return (cc <= rc) & (cc > rc - n_band)


def _bias_prep_kernel(tab_ref, o_ref):
    h = pl.program_id(0)
    width = B_REACH + SUB
    ulen = width + SUB
    n = lax.broadcasted_iota(jnp.int32, (8, ulen), 1)
    idx = jnp.clip(B_REACH + SUB - 1 - n, -REL_CLIP, REL_CLIP) + REL_CLIP

    def body(m, u):
        return jnp.where(idx == m, tab_ref[h, m], u)

    u = lax.fori_loop(0, N_REL, body, jnp.zeros((8, ulen), F32))
    x = jnp.broadcast_to(u[0:1, :], (SUB, ulen))
    x = pltpu.roll(x, ulen - (SUB - 1), axis=1, stride=1, stride_axis=0)
    o_ref[0] = jnp.where(_band_mask(SUB, B_REACH, B_BAND_CHUNKS), x[:, :width], NEG_INF)


def bias_prep(table):
    return pl.pallas_call(
        _bias_prep_kernel,
        grid=(HB,),
        in_specs=[pl.BlockSpec(memory_space=pltpu.SMEM)],
        out_specs=pl.BlockSpec((1, SUB, B_REACH + SUB), lambda h: (h, 0, 0)),
        out_shape=jax.ShapeDtypeStruct((HB, SUB, B_REACH + SUB), F32),
        compiler_params=_params(("arbitrary",)),
        name="bias_prep",
    )(table)


def _band_prompt_kernel(*refs, group, halo, n_band, has_bias, has_sink):
    q_ref, km_ref, kh_ref, vm_ref, vh_ref = refs[:5]
    rest = list(refs[5:])
    bias_ref = rest.pop(0) if has_bias else None
    sink_ref = rest.pop(0) if has_sink else None
    o_ref = rest.pop(0)
    h = pl.program_id(0)
    i = pl.program_id(1)
    qb = q_ref.shape[0]
    width = halo + SUB
    k = jnp.concatenate([kh_ref[...], km_ref[...]], axis=0)
    v = jnp.concatenate([vh_ref[...], vm_ref[...]], axis=0)
    c = lax.broadcasted_iota(jnp.int32, (SUB, width), 1)
    if has_bias:
        bias = bias_ref[0]
    else:
        bias = jnp.where(_band_mask(SUB, halo, n_band), 0.0, NEG_INF).astype(F32)
    for sb in range(qb // SUB):
        kw = k[sb * SUB:sb * SUB + width]
        vw = v[sb * SUB:sb * SUB + width]
        first_valid = halo - i * qb - sb * SUB
        for g in range(group):
            q = q_ref[sb * SUB:(sb + 1) * SUB, g * HEAD_DIM:(g + 1) * HEAD_DIM]
            s = _dot_t(q, kw) + bias
            s = jnp.where(c >= first_valid, s, NEG_INF)
            m = jnp.max(s, axis=-1, keepdims=True)
            if has_sink:
                sk = sink_ref[h * group + g]
                m = jnp.maximum(m, sk)
            p = jnp.exp(s - m)
            l = jnp.sum(p, axis=-1, keepdims=True)
            if has_sink:
                l = l + jnp.exp(sk - m)
            o = _dot(p.astype(BF16), vw) / l
            o_ref[sb * SUB:(sb + 1) * SUB, g * HEAD_DIM:(g + 1) * HEAD_DIM] = o.astype(o_ref.dtype)


def band_prompt(proj, *, q_col, k_col, v_col, n_kv, group, halo, n_band, bias=None, sink=None):
    s = proj.shape[0]
    qb = BAND_QB
    assert s % qb == 0 and qb % halo == 0 and halo % CHUNK == 0
    qw = group * HEAD_DIM
    hpb = qb // halo
    qc, kc, vc = q_col // qw, k_col // HEAD_DIM, v_col // HEAD_DIM
    main = lambda c0: pl.BlockSpec((qb, HEAD_DIM), lambda h, i: (i, c0 + h))
    halo_spec = lambda c0: pl.BlockSpec((halo, HEAD_DIM), lambda h, i: (jnp.maximum(i * hpb - 1, 0), c0 + h))
    in_specs = [pl.BlockSpec((qb, qw), lambda h, i: (i, qc + h)), main(kc), halo_spec(kc), main(vc), halo_spec(vc)]
    args = [proj, proj, proj, proj, proj]
    if bias is not None:
        in_specs.append(pl.BlockSpec((1, SUB, halo + SUB), lambda h, i: (h, 0, 0)))
        args.append(bias)
    if sink is not None:
        in_specs.append(pl.BlockSpec(memory_space=pltpu.SMEM))
        args.append(sink)
    return pl.pallas_call(
        functools.partial(_band_prompt_kernel, group=group, halo=halo, n_band=n_band,
                          has_bias=bias is not None, has_sink=sink is not None),
        grid=(n_kv, s // qb),
        in_specs=in_specs,
        out_specs=pl.BlockSpec((qb, qw), lambda h, i: (i, h)),
        out_shape=jax.ShapeDtypeStruct((s, n_kv * qw), BF16),
        compiler_params=_params(("parallel", "arbitrary")),
        name="band_prompt",
    )(*args)


def _band_sample_kernel(*refs, group, n_kv, has_bias, has_sink):
    q_ref, kn_ref, vn_ref, kc_ref, vc_ref = refs[:5]
    rest = list(refs[5:])
    bias_ref = rest.pop(0) if has_bias else None
    sink_ref = rest.pop(0) if has_sink else None
    o_ref, ko_ref, vo_ref = rest[-3:]
    t = q_ref.shape[0]
    lc = kc_ref.shape[0]
    ko_ref[:lc - t] = kc_ref[t:]
    vo_ref[:lc - t] = vc_ref[t:]
    if has_sink:
        row = lax.broadcasted_iota(jnp.int32, (group * t, 1), 0)
    for h in range(n_kv):
        cols = slice(h * HEAD_DIM, (h + 1) * HEAD_DIM)
        kn = kn_ref[:, cols]
        vn = vn_ref[:, cols]
        ko_ref[lc - t:, h, :] = kn
        vo_ref[lc - t:, h, :] = vn
        kc = kc_ref[:, h, :].astype(BF16)
        vc = vc_ref[:, h, :].astype(BF16)
        heads = [h * group + g for g in range(group)]
        q = jnp.concatenate([q_ref[:, j * HEAD_DIM:(j + 1) * HEAD_DIM] for j in heads], axis=0).astype(BF16)
        sc = _dot_t(q, kc)
        sn = _dot_t(q, kn.astype(BF16))
        if has_bias:
            sc = sc + bias_ref[h, :t, :lc]
            sn = sn + bias_ref[h, :t, lc:lc + t]
        m = jnp.maximum(jnp.max(sc, axis=-1, keepdims=True), jnp.max(sn, axis=-1, keepdims=True))
        if has_sink:
            sk = jnp.zeros((group * t, 1), F32)
            for g, j in enumerate(heads):
                sk = jnp.where((row >= g * t) & (row < (g + 1) * t), sink_ref[j], sk)
            m = jnp.maximum(m, sk)
        pc = jnp.exp(sc - m)
        pn = jnp.exp(sn - m)
        l = jnp.sum(pc, axis=-1, keepdims=True) + jnp.sum(pn, axis=-1, keepdims=True)
        if has_sink:
            l = l + jnp.exp(sk - m)
        o = (_dot(pc.astype(BF16), vc) + _dot(pn.astype(BF16), vn.astype(BF16))) / l
        for g, j in enumerate(heads):
            o_ref[:, j * HEAD_DIM:(j + 1) * HEAD_DIM] = o[g * t:(g + 1) * t].astype(o_ref.dtype)


def band_sample(q, k_new, v_new, cache_k, cache_v, stack_k, stack_v, layer, *, t, group, bias=None, sink=None):
    _, nb, lc, n_kv, _ = cache_k.shape
    rows = lambda width: pl.BlockSpec((t, width), lambda b: (b, 0))
    cache_spec = pl.BlockSpec((None, None, lc, n_kv, HEAD_DIM), lambda b: (layer, b, 0, 0, 0))
    in_specs = [rows(q.shape[1]), rows(k_new.shape[1]), rows(v_new.shape[1]), cache_spec, cache_spec]
    args = [q, k_new, v_new, cache_k, cache_v]
    if bias is not None:
        assert lc == B_REACH and t <= CHUNK
        in_specs.append(pl.BlockSpec(bias.shape, lambda b: (0, 0, 0)))
        args.append(bias)
    if sink is not None:
        in_specs.append(pl.BlockSpec(memory_space=pltpu.SMEM))
        args.append(sink)
    aliases = {}
    if stack_k is not None:
        aliases = {len(args): 1, len(args) + 1: 2}
        in_specs += [pl.BlockSpec(memory_space=pl.ANY)] * 2
        args += [stack_k, stack_v]
    return pl.pallas_call(
        functools.partial(_band_sample_kernel, group=group, n_kv=n_kv, has_bias=bias is not None,
                          has_sink=sink is not None),
        grid=(nb,),
        in_specs=in_specs,
        out_specs=[rows(q.shape[1]), cache_spec, cache_spec],
        out_shape=[jax.ShapeDtypeStruct(q.shape, BF16),
                   jax.ShapeDtypeStruct(cache_k.shape, F32), jax.ShapeDtypeStruct(cache_v.shape, F32)],
        input_output_aliases=aliases,
        compiler_params=_params(("parallel",)),
        name="band_sample",
    )(*args)


def _tri2():
    j = lax.broadcasted_iota(jnp.int32, (SB_BLOCK, 2 * SB_BLOCK), 0)
    s = lax.broadcasted_iota(jnp.int32, (SB_BLOCK, 2 * SB_BLOCK), 1)
    return jnp.where((s >= SB_BLOCK) | (j > s), 1.0, 0.0).astype(BF16)


def _softplus(z):
    neg_abs = lax.bitcast_convert_type(lax.bitcast_convert_type(z, jnp.int32) | jnp.int32(-2 ** 31), F32)
    return jnp.maximum(z, 0.0) + jnp.log(1.0 + jnp.exp(neg_abs))


def _sb_tile(z, tri2, carry):
    nls = _softplus(z)
    la = _dot(nls.astype(BF16), tri2)
    logw = z - nls - la[:, :SB_BLOCK]
    if carry is None:
        return logw, la[:, SB_BLOCK:]
    return logw - carry, carry + la[:, SB_BLOCK:]


def _causal_tile_mask(rows):
    r = lax.broadcasted_iota(jnp.int32, (rows, SB_BLOCK), 0)
    c = lax.broadcasted_iota(jnp.int32, (rows, SB_BLOCK), 1)
    return c < r


def _sb_block(ref, j):
    return ref[pl.ds(pl.multiple_of(j * SB_BLOCK, SB_BLOCK), SB_BLOCK), :].astype(BF16)


def _sb_older_blocks(q, k_ref, v_ref, tri2, first, acc, carry):
    def cond(state):
        j, _, carry = state
        return (j >= 0) & (jnp.min(carry) < SB_SKIP_LOG)

    def body(state):
        j, acc, carry = state
        logw, carry = _sb_tile(_dot_t(q, _sb_block(k_ref, j)), tri2, carry)
        acc = acc + _dot(jnp.exp(logw).astype(BF16), _sb_block(v_ref, j))
        return j - 1, acc, carry

    _, acc, _ = lax.while_loop(cond, body, (first, acc, carry))
    return acc


def _sb_prompt_kernel(q_ref, k_ref, v_ref, o_ref):
    tq = q_ref.shape[0]
    nsub = tq // SB_BLOCK
    nwin = nsub + SB_LOOKBACK
    pid = pl.program_id(1)
    base = pid * nsub
    is_first = pid == 0
    start = pl.multiple_of(jnp.maximum(base - SB_LOOKBACK, 0) * SB_BLOCK, SB_BLOCK)
    q = q_ref[...]
    kw = k_ref[pl.ds(start, nwin * SB_BLOCK), :]
    vw = v_ref[pl.ds(start, nwin * SB_BLOCK), :]
    kw = jnp.where(is_first, jnp.roll(kw, SB_LOOKBACK * SB_BLOCK, axis=0), kw)
    vw = jnp.where(is_first, jnp.roll(vw, SB_LOOKBACK * SB_BLOCK, axis=0), vw)
    tri2 = _tri2()
    z = _dot_t(q, kw)
    diag = _causal_tile_mask(SB_BLOCK)
    accs, carries = [], []
    for a in range(nsub):
        rows = slice(a * SB_BLOCK, (a + 1) * SB_BLOCK)
        carry = None
        pieces = []
        for b in range(a + SB_LOOKBACK, -1, -1):
            zt = z[rows, b * SB_BLOCK:(b + 1) * SB_BLOCK]
            if b == a + SB_LOOKBACK:
                zt = jnp.where(diag, zt, NEG_INF)
            if b < SB_LOOKBACK:
                zt = jnp.where(is_first, NEG_INF, zt)
            logw, carry = _sb_tile(zt, tri2, carry)
            pieces.insert(0, logw)
        w = jnp.exp(jnp.concatenate(pieces, axis=1)).astype(BF16)
        accs.append(_dot(w, vw[:(a + SB_LOOKBACK + 1) * SB_BLOCK]))
        carries.append(carry)
    acc = jnp.concatenate(accs, axis=0)
    carry = jnp.concatenate(carries, axis=0)
    acc = _sb_older_blocks(q, k_ref, v_ref, tri2, base - SB_LOOKBACK - 1, acc, carry)
    o_ref[...] = acc.astype(o_ref.dtype)


def sb_prompt(q, k, v):
    s = q.shape[0]
    tq = min(SB_QUERY_ROWS, s)
    assert s % tq == 0 and tq % SB_BLOCK == 0 and s >= tq + SB_LOOKBACK * SB_BLOCK
    head_all = pl.BlockSpec((s, HEAD_DIM), lambda h, i: (0, h))
    return pl.pallas_call(
        _sb_prompt_kernel,
        grid=(HC, s // tq),
        in_specs=[pl.BlockSpec((tq, HEAD_DIM), lambda h, i: (i, h)), head_all, head_all],
        out_specs=pl.BlockSpec((tq, HEAD_DIM), lambda h, i: (i, h)),
        out_shape=jax.ShapeDtypeStruct(q.shape, BF16),
        compiler_params=_params(("parallel", "arbitrary")),
        name="sb_prompt",
    )(q, k, v)


def _sb_sample_kernel(q_ref, kn_ref, vn_ref, kc_ref, vc_ref, o_ref, acc_ref, carry_ref):
    s = pl.program_id(1)
    t = q_ref.shape[0]
    n_heads = acc_ref.shape[0]
    tri2 = _tri2()

    def head_q(h):
        return q_ref[:, h * HEAD_DIM:(h + 1) * HEAD_DIM].astype(BF16)

    @pl.when(s == 0)
    def _():
        zeros = jnp.zeros((SB_BLOCK - t, HEAD_DIM), BF16)
        mask = _causal_tile_mask(t)
        for h in range(n_heads):
            cols = slice(h * HEAD_DIM, (h + 1) * HEAD_DIM)
            kb = jnp.concatenate([kn_ref[:, cols].astype(BF16), zeros], axis=0)
            vb = jnp.concatenate([vn_ref[:, cols].astype(BF16), zeros], axis=0)
            logw, carry = _sb_tile(jnp.where(mask, _dot_t(head_q(h), kb), NEG_INF), tri2, None)
            acc_ref[h] = _dot(jnp.exp(logw).astype(BF16), vb)
            carry_ref[h] = carry

    @pl.when(jnp.min(carry_ref[...]) < SB_SKIP_LOG)
    def _():
        for h in range(n_heads):
            kb = kc_ref[:, h, :].astype(BF16)
            vb = vc_ref[:, h, :].astype(BF16)
            logw, carry = _sb_tile(_dot_t(head_q(h), kb), tri2, carry_ref[h])
            acc_ref[h] += _dot(jnp.exp(logw).astype(BF16), vb)
            carry_ref[h] = carry

    @pl.when(s == pl.num_programs(1) - 1)
    def _():
        for h in range(n_heads):
            o_ref[:, h * HEAD_DIM:(h + 1) * HEAD_DIM] = acc_ref[h].astype(o_ref.dtype)


def sb_sample(q, k_new, v_new, cache_k, cache_v, layer, *, t):
    _, nb, past, n_heads, _ = cache_k.shape
    assert past % SB_BLOCK == 0 and t <= SB_BLOCK
    nblk = past // SB_BLOCK
    rows = pl.BlockSpec((t, n_heads * HEAD_DIM), lambda b, s: (b, 0))
    cache_spec = pl.BlockSpec((None, None, SB_BLOCK, n_heads, HEAD_DIM), lambda b, s: (layer, b, nblk - 1 - s, 0, 0))
    return pl.pallas_call(
        _sb_sample_kernel,
        grid=(nb, nblk),
        in_specs=[rows, rows, rows, cache_spec, cache_spec],
        out_specs=rows,
        out_shape=jax.ShapeDtypeStruct(q.shape, BF16),
        scratch_shapes=[pltpu.VMEM((n_heads, t, HEAD_DIM), F32), pltpu.VMEM((n_heads, t, HEAD_DIM), F32)],
        compiler_params=_params(("parallel", "arbitrary")),
        name="sb_sample",
    )(q, k_new, v_new, cache_k, cache_v)


def _rope_table_kernel(inv_ref, cos_ref, sin_ref, *, pos0):
    tm = cos_ref.shape[0]
    pos = lax.broadcasted_iota(jnp.int32, (tm, HEAD_DIM), 0) + (pos0 + pl.program_id(0) * tm)
    lane = lax.broadcasted_iota(jnp.int32, (tm, HEAD_DIM), 1)
    ang = pos.astype(F32) * inv_ref[...]
    sin = jnp.sin(ang)
    cos_ref[...] = jnp.cos(ang)
    sin_ref[...] = jnp.where(lane < HEAD_DIM // 2, -sin, sin)


def _rope_tables(n, pos0):
    half = HEAD_DIM // 2
    inv = ROPE_THETA ** (-jnp.arange(half, dtype=F32) / half)
    inv = jnp.concatenate([inv, inv]).reshape(1, HEAD_DIM)
    tm = min(1024, n)
    assert n % tm == 0
    return pl.pallas_call(
        functools.partial(_rope_table_kernel, pos0=pos0),
        grid=(n // tm,),
        in_specs=[pl.BlockSpec((1, HEAD_DIM), lambda i: (0, 0))],
        out_specs=[pl.BlockSpec((tm, HEAD_DIM), lambda i: (i, 0))] * 2,
        out_shape=[jax.ShapeDtypeStruct((n, HEAD_DIM), F32)] * 2,
        compiler_params=_params(("parallel",)),
        name="rope_tables",
    )(inv)


def kernel(x_prompt, x_sample, cache_a_k, cache_a_v, cache_b_k, cache_b_v, cache_c_k, cache_c_v, norm_mix, w_in_ab, sink_a, rel_bias_b, w_out_ab, w_in_c, w_out_c, norm_ffn, w_gate, w_up, w_down, norm_final):
    bp, seq, d = x_prompt.shape
    nb, t, _ = x_sample.shape
    depth = norm_mix.shape[0]
    past = cache_c_k.shape[2]
    assert bp == 1 and seq >= B_REACH
    keep_a, keep_b = min(A_REACH, seq), min(B_REACH, seq)
    wc = HC * HEAD_DIM

    yp = x_prompt.reshape(seq, d)
    ys = x_sample.reshape(nb * t, d)
    cos_p, sin_p = _rope_tables(seq, 0)
    cos_s, sin_s = _rope_tables(t, past)
    cos_s, sin_s = jnp.tile(cos_s, (nb, 1)), jnp.tile(sin_s, (nb, 1))

    outs = {name: [] for name in ("pa_k", "pa_v", "pb_k", "pb_v", "sc_k", "sc_v")}
    n_c = cache_c_k.shape[0]
    sa_k = sa_v = sb_k = sb_v = pc_k = pc_v = None
    hp, qp = norm_inputs(yp)
    hs, qs_ = norm_inputs(ys)
    for layer in range(depth):
        i = layer // 2
        if layer % 2 == 0:
            w_in = cast_bf16(w_in_ab, i, norm_mix[layer])
            w_out = cast_bf16(w_out_ab, i)
            bias = bias_prep(rel_bias_b[i])
            proj = matmul_ab(hp, qp, w_in, cos_p, sin_p, BF16)
            oa = band_prompt(proj, q_col=0, k_col=AB_SPLITS[0], v_col=AB_SPLITS[1], n_kv=KV_A, group=G_A,
                             halo=A_REACH, n_band=A_BAND_CHUNKS, sink=sink_a[i])
            ob = band_prompt(proj, q_col=AB_SPLITS[2], k_col=AB_SPLITS[3], v_col=AB_SPLITS[4], n_kv=HB, group=1,
                             halo=B_REACH, n_band=B_BAND_CHUNKS, bias=bias)
            tail = matmul_ab(hp[seq - keep_b:], qp[seq - keep_b:], w_in, cos_p[seq - keep_b:], sin_p[seq - keep_b:], F32)
            yp, hp, qp = matmul2_res(oa, ob, w_out, yp)
            outs["pa_k"].append(tail[keep_b - keep_a:, AB_SPLITS[0]:AB_SPLITS[1]].reshape(1, keep_a, KV_A, HEAD_DIM))
            outs["pa_v"].append(tail[keep_b - keep_a:, AB_SPLITS[1]:AB_SPLITS[2]].reshape(1, keep_a, KV_A, HEAD_DIM))
            outs["pb_k"].append(tail[:, AB_SPLITS[3]:AB_SPLITS[4]].reshape(1, keep_b, HB, HEAD_DIM))
            outs["pb_v"].append(tail[:, AB_SPLITS[4]:].reshape(1, keep_b, HB, HEAD_DIM))
            projs = matmul_ab(hs, qs_, w_in, cos_s, sin_s, F32)
            qa, ka, va, qb, kb, vb = jnp.split(projs, AB_SPLITS, axis=1)
            oa, sa_k, sa_v = band_sample(qa, ka, va, cache_a_k, cache_a_v, sa_k, sa_v, i, t=t, group=G_A,
                                         sink=sink_a[i])
            ob, sb_k, sb_v = band_sample(qb, kb, vb, cache_b_k, cache_b_v, sb_k, sb_v, i, t=t, group=1, bias=bias)
            ys, hs, qs_ = matmul2_res(oa, ob, w_out, ys)
        else:
            w_in = cast_bf16(w_in_c, i, norm_mix[layer])
            w_out = cast_bf16(w_out_c, i)
            q = matmul(hp, qp, w_in, BF16, scale=QK_SCALE, col0=0, n=wc)
            pc_k, k16 = matmul_heads(hp, qp, w_in, pc_k, i, n_c, col0=wc, n=wc)
            pc_v, v16 = matmul_heads(hp, qp, w_in, pc_v, i, n_c, col0=2 * wc, n=wc)
            att = sb_prompt(q, k16, v16)
            yp, hp, qp = matmul_res(att, w_out, yp, tm=1024, tn=512)
            q = matmul(hs, qs_, w_in, F32, scale=QK_SCALE, col0=0, n=wc)
            ks = matmul(hs, qs_, w_in, F32, col0=wc, n=wc)
            vs = matmul(hs, qs_, w_in, F32, col0=2 * wc, n=wc)
            att = sb_sample(q, ks, vs, cache_c_k, cache_c_v, i, t=t)
            ys, hs, qs_ = matmul_res(att, w_out, ys, tm=1024, tn=512)
            outs["sc_k"].append(ks.reshape(nb, t, HC, HEAD_DIM))
            outs["sc_v"].append(vs.reshape(nb, t, HC, HEAD_DIM))
        wg = cast_bf16(w_gate, layer, norm_ffn[layer])
        wu = cast_bf16(w_up, layer, norm_ffn[layer])
        wd = cast_bf16(w_down, layer)
        yp, hp, qp = matmul_res(gateup(hp, qp, wg, wu), wd, yp)
        ys, hs, qs_ = matmul_res(gateup(hs, qs_, wg, wu), wd, ys)

    y_prompt = rmsnorm(yp, norm_final, F32).reshape(1, seq, d)
    y_sample = rmsnorm(ys, norm_final, F32).reshape(nb, t, d)
    st = {name: jnp.stack(v) for name, v in outs.items()}
    pc_k = pc_k.reshape(n_c, 1, seq, HC, HEAD_DIM)
    pc_v = pc_v.reshape(n_c, 1, seq, HC, HEAD_DIM)
    return (y_prompt, y_sample, st["pa_k"], st["pa_v"], st["pb_k"], st["pb_v"], pc_k, pc_v,
            sa_k, sa_v, sb_k, sb_v, st["sc_k"], st["sc_v"])
```

```python
import functools

import jax
import jax.numpy as jnp
from jax import lax
from jax.experimental import pallas as pl
from jax.experimental.pallas import tpu as pltpu

HEAD_DIM = 128
CHUNK = 64
HA = 16
KV_A = 4
G_A = HA // KV_A
HB = 16
HC = 16
A_BAND_CHUNKS = 3
B_BAND_CHUNKS = 9
A_REACH = (A_BAND_CHUNKS - 1) * CHUNK
B_REACH = (B_BAND_CHUNKS - 1) * CHUNK
REL_CLIP = 256
N_REL = 2 * REL_CLIP + 1
SB_BLOCK = 128
ROPE_THETA = 10000.0
RMS_EPS = 1e-6
NEG_INF = -1e30
QA_W = HA * HEAD_DIM
KA_W = KV_A * HEAD_DIM
QB_W = HB * HEAD_DIM
AB_SPLITS = (QA_W, QA_W + KA_W, QA_W + 2 * KA_W, QA_W + 2 * KA_W + QB_W, QA_W + 2 * KA_W + 2 * QB_W)
QK_SCALE = HEAD_DIM ** -0.5

SB_QUERY_ROWS = 512
SB_LOOKBACK = 2
SB_SKIP_LOG = 106.0
AB_TILE = 512
OUT_TILE = 512
FF_TILE = 256
SUB = 128
BAND_QB = 512
VMEM_LIMIT_MB = 56

F32 = jnp.float32
BF16 = jnp.bfloat16


def _params(semantics):
    return pltpu.CompilerParams(dimension_semantics=semantics, vmem_limit_bytes=VMEM_LIMIT_MB << 20)


def _dot(a, b):
    return jnp.dot(a, b, preferred_element_type=F32)


def _dot_t(a, b):
    return lax.dot_general(a, b, (((1,), (1,)), ((), ())), preferred_element_type=F32)


def _rmsnorm_kernel(x_ref, g_ref, o_ref):
    x = x_ref[...]
    ms = jnp.mean(x * x, axis=-1, keepdims=True)
    o_ref[...] = (x * lax.rsqrt(ms + RMS_EPS) * g_ref[...]).astype(o_ref.dtype)


def rmsnorm(x, g, out_dtype):
    m, d = x.shape
    tm = min(256, m)
    return pl.pallas_call(
        _rmsnorm_kernel,
        grid=(m // tm,),
        in_specs=[pl.BlockSpec((tm, d), lambda i: (i, 0)), pl.BlockSpec((1, d), lambda i: (0, 0))],
        out_specs=pl.BlockSpec((tm, d), lambda i: (i, 0)),
        out_shape=jax.ShapeDtypeStruct((m, d), out_dtype),
        compiler_params=_params(("parallel",)),
        name="rmsnorm",
    )(x, g.reshape(1, d))


def _lane_partial_ssq(y):
    sq = y * y
    part = sq[:, :HEAD_DIM]
    for c in range(1, y.shape[1] // HEAD_DIM):
        part = part + sq[:, c * HEAD_DIM:(c + 1) * HEAD_DIM]
    return part


def _row_rstd(ssq_ref, d):
    return lax.rsqrt(jnp.sum(ssq_ref[...], axis=-1, keepdims=True) / d + RMS_EPS)


def _norm_inputs_kernel(x_ref, o16_ref, ssq_ref):
    x = x_ref[...]
    o16_ref[...] = x.astype(o16_ref.dtype)
    ssq_ref[...] = _lane_partial_ssq(x)


def norm_inputs(x):
    m, d = x.shape
    tm = min(256, m)
    return pl.pallas_call(
        _norm_inputs_kernel,
        grid=(m // tm,),
        in_specs=[pl.BlockSpec((tm, d), lambda i: (i, 0))],
        out_specs=[pl.BlockSpec((tm, d), lambda i: (i, 0)), pl.BlockSpec((tm, HEAD_DIM), lambda i: (i, 0))],
        out_shape=[jax.ShapeDtypeStruct((m, d), BF16), jax.ShapeDtypeStruct((m, HEAD_DIM), F32)],
        compiler_params=_params(("parallel",)),
        name="norm_inputs",
    )(x)


def _cast_kernel(x_ref, *rest):
    o_ref = rest[-1]
    x = x_ref[...]
    if len(rest) == 2:
        x = x * rest[0][...]
    x = x.astype(o_ref.dtype)
    if len(o_ref.shape) == 2:
        o_ref[...] = x
    else:
        ct = o_ref.shape[2]
        for c in range(o_ref.shape[0]):
            o_ref[c] = x[:, c * ct:(c + 1) * ct]


def cast_bf16(w_stack, layer, gain=None, col_tile=None):
    _, k, n = w_stack.shape
    tr = min(256, k)
    assert k % tr == 0
    in_specs = [pl.BlockSpec((None, tr, n), lambda i: (layer, i, 0))]
    args = [w_stack]
    if gain is not None:
        in_specs.append(pl.BlockSpec((tr, 1), lambda i: (i, 0)))
        args.append(gain.reshape(k, 1))
    if col_tile is None:
        out_spec = pl.BlockSpec((tr, n), lambda i: (i, 0))
        out_shape = jax.ShapeDtypeStruct((k, n), BF16)
    else:
        assert n % col_tile == 0
        out_spec = pl.BlockSpec((n // col_tile, tr, col_tile), lambda i: (0, i, 0))
        out_shape = jax.ShapeDtypeStruct((n // col_tile, k, col_tile), BF16)
    return pl.pallas_call(
        _cast_kernel,
        grid=(k // tr,),
        in_specs=in_specs,
        out_specs=out_spec,
        out_shape=out_shape,
        compiler_params=_params(("parallel",)),
        name="cast_bf16",
    )(*args)


def _w_spec(w, k, tn, c0=0):
    if w.ndim == 2:
        return pl.BlockSpec((k, tn), lambda i, j: (0, c0 + j))
    assert w.shape[1:] == (k, tn), (w.shape, k, tn)
    return pl.BlockSpec((None, k, tn), lambda i, j: (c0 + j, 0, 0))


def _w_cols(w):
    return w.shape[1] if w.ndim == 2 else w.shape[0] * w.shape[2]


def _mm_kernel(a_ref, ssq_ref, w_ref, o_ref, *, scale):
    acc = _dot(a_ref[...], w_ref[...]) * (_row_rstd(ssq_ref, a_ref.shape[1]) * scale)
    o_ref[...] = acc.astype(o_ref.dtype)


def _mm_heads_kernel(a_ref, ssq_ref, w_ref, *rest):
    o32_ref, o16_ref = rest[-2:]
    acc = _dot(a_ref[...], w_ref[...]) * _row_rstd(ssq_ref, a_ref.shape[1])
    for h in range(o32_ref.shape[1]):
        o32_ref[:, h, :] = acc[:, h * HEAD_DIM:(h + 1) * HEAD_DIM]
    o16_ref[...] = acc.astype(o16_ref.dtype)


def _emit_residual(y, o_ref, o16_ref, ssq_ref):
    o_ref[...] = y
    o16_ref[...] = y.astype(o16_ref.dtype)
    part = _lane_partial_ssq(y)
    first = pl.program_id(1) == 0

    @pl.when(first)
    def _():
        ssq_ref[...] = part

    @pl.when(jnp.logical_not(first))
    def _():
        ssq_ref[...] += part


def _mm_res_kernel(a_ref, w_ref, r_ref, o_ref, o16_ref, ssq_ref):
    _emit_residual(r_ref[...] + _dot(a_ref[...], w_ref[...]), o_ref, o16_ref, ssq_ref)


def _mm2_res_kernel(a1_ref, a2_ref, w_ref, r_ref, o_ref, o16_ref, ssq_ref):
    k1 = a1_ref.shape[1]
    acc = _dot(a1_ref[...], w_ref[:k1, :]) + _dot(a2_ref[...], w_ref[k1:, :])
    _emit_residual(r_ref[...] + acc, o_ref, o16_ref, ssq_ref)


def _gateup_kernel(a_ref, ssq_ref, wg_ref, wu_ref, o_ref):
    a = a_ref[...]
    rstd = _row_rstd(ssq_ref, a_ref.shape[1])
    g = _dot(a, wg_ref[...]) * rstd
    u = _dot(a, wu_ref[...]) * rstd
    o_ref[...] = (g / (1.0 + jnp.exp(-g)) * u).astype(o_ref.dtype)


def _mm_ab_kernel(a_ref, ssq_ref, w_ref, cos_ref, sin_ref, o_ref, *, tn):
    col0 = pl.program_id(1) * tn
    acc = _dot(a_ref[...], w_ref[...]) * _row_rstd(ssq_ref, a_ref.shape[1])
    is_rope = col0 < AB_SPLITS[1]
    is_q = (col0 < AB_SPLITS[0]) | ((col0 >= AB_SPLITS[2]) & (col0 < AB_SPLITS[3]))
    s = jnp.where(is_q, QK_SCALE, 1.0).astype(F32)

    @pl.when(is_rope)
    def _():
        cos = cos_ref[...]
        sin = sin_ref[...]
        for g in range(tn // HEAD_DIM):
            blk = acc[:, g * HEAD_DIM:(g + 1) * HEAD_DIM]
            rot = blk * cos + pltpu.roll(blk, HEAD_DIM // 2, axis=1) * sin
            o_ref[:, g * HEAD_DIM:(g + 1) * HEAD_DIM] = (rot * s).astype(o_ref.dtype)

    @pl.when(jnp.logical_not(is_rope))
    def _():
        o_ref[...] = (acc * s).astype(o_ref.dtype)


def _tiles(m, n, tm, tn):
    tm = min(tm, m)
    tn = min(tn, n)
    assert m % tm == 0 and n % tn == 0, (m, n, tm, tn)
    return tm, tn


def _ssq_spec(tm):
    return pl.BlockSpec((tm, HEAD_DIM), lambda i, j: (i, 0))


def _residual_out_specs(tm, tn):
    return [pl.BlockSpec((tm, tn), lambda i, j: (i, j)), pl.BlockSpec((tm, tn), lambda i, j: (i, j)), _ssq_spec(tm)]


def _residual_out_shapes(m, n):
    return [jax.ShapeDtypeStruct((m, n), F32), jax.ShapeDtypeStruct((m, n), BF16),
            jax.ShapeDtypeStruct((m, HEAD_DIM), F32)]


def matmul(a, ssq, w, out_dtype, *, scale=1.0, col0=0, n=None, tm=1024, tn=512):
    m, k = a.shape
    n = w.shape[1] - col0 if n is None else n
    tm, tn = _tiles(m, n, tm, tn)
    assert col0 % tn == 0
    c0 = col0 // tn
    return pl.pallas_call(
        functools.partial(_mm_kernel, scale=scale),
        grid=(m // tm, n // tn),
        in_specs=[pl.BlockSpec((tm, k), lambda i, j: (i, 0)), _ssq_spec(tm),
                  pl.BlockSpec((k, tn), lambda i, j: (0, c0 + j))],
        out_specs=pl.BlockSpec((tm, tn), lambda i, j: (i, j)),
        out_shape=jax.ShapeDtypeStruct((m, n), out_dtype),
        compiler_params=_params(("parallel", "arbitrary")),
        name="matmul",
    )(a, ssq, w)


def matmul_heads(a, ssq, w, stack, layer, n_layers, *, col0, n, tm=1024, tn=1024):
    m, k = a.shape
    tm, tn = _tiles(m, n, tm, tn)
    assert col0 % tn == 0 and tn % (8 * HEAD_DIM) == 0
    c0 = col0 // tn
    hpt = tn // HEAD_DIM
    in_specs = [pl.BlockSpec((tm, k), lambda i, j: (i, 0)), _ssq_spec(tm),
                pl.BlockSpec((k, tn), lambda i, j: (0, c0 + j))]
    args = [a, ssq, w]
    if stack is not None:
        in_specs.append(pl.BlockSpec(memory_space=pl.ANY))
        args.append(stack)
    return pl.pallas_call(
        _mm_heads_kernel,
        grid=(m // tm, n // tn),
        in_specs=in_specs,
        out_specs=[pl.BlockSpec((None, tm, hpt, HEAD_DIM), lambda i, j: (layer, i, j, 0)),
                   pl.BlockSpec((tm, tn), lambda i, j: (i, j))],
        out_shape=[jax.ShapeDtypeStruct((n_layers, m, n // HEAD_DIM, HEAD_DIM), F32), jax.ShapeDtypeStruct((m, n), BF16)],
        input_output_aliases={} if stack is None else {3: 0},
        compiler_params=_params(("parallel", "arbitrary")),
        name="matmul_heads",
    )(*args)


def matmul_res(a, w, res, *, tm=512, tn=OUT_TILE):
    m, k = a.shape
    n = _w_cols(w)
    tm, tn = _tiles(m, n, tm, tn)
    return pl.pallas_call(
        _mm_res_kernel,
        grid=(m // tm, n // tn),
        in_specs=[pl.BlockSpec((tm, k), lambda i, j: (i, 0)), _w_spec(w, k, tn),
                  pl.BlockSpec((tm, tn), lambda i, j: (i, j))],
        out_specs=_residual_out_specs(tm, tn),
        out_shape=_residual_out_shapes(m, n),
        compiler_params=_params(("parallel", "arbitrary")),
        name="matmul_res",
    )(a, w, res)


def matmul2_res(a1, a2, w, res, *, tm=1024, tn=OUT_TILE):
    m, k1 = a1.shape
    k2 = a2.shape[1]
    n = _w_cols(w)
    tm, tn = _tiles(m, n, tm, tn)
    return pl.pallas_call(
        _mm2_res_kernel,
        grid=(m // tm, n // tn),
        in_specs=[pl.BlockSpec((tm, k1), lambda i, j: (i, 0)), pl.BlockSpec((tm, k2), lambda i, j: (i, 0)),
                  _w_spec(w, k1 + k2, tn), pl.BlockSpec((tm, tn), lambda i, j: (i, j))],
        out_specs=_residual_out_specs(tm, tn),
        out_shape=_residual_out_shapes(m, n),
        compiler_params=_params(("parallel", "arbitrary")),
        name="matmul2_res",
    )(a1, a2, w, res)


def gateup(a, ssq, wg, wu, *, tm=2048, tn=FF_TILE):
    m, k = a.shape
    n = _w_cols(wg)
    tm, tn = _tiles(m, n, tm, tn)
    return pl.pallas_call(
        _gateup_kernel,
        grid=(m // tm, n // tn),
        in_specs=[pl.BlockSpec((tm, k), lambda i, j: (i, 0)), _ssq_spec(tm), _w_spec(wg, k, tn), _w_spec(wu, k, tn)],
        out_specs=pl.BlockSpec((tm, tn), lambda i, j: (i, j)),
        out_shape=jax.ShapeDtypeStruct((m, n), BF16),
        compiler_params=_params(("parallel", "arbitrary")),
        name="gateup",
    )(a, ssq, wg, wu)


def matmul_ab(a, ssq, w, cos, sin, out_dtype, *, tm=2048):
    m, k = a.shape
    n = _w_cols(w)
    tn = AB_TILE
    assert all(s % tn == 0 for s in AB_SPLITS)
    tm, tn = _tiles(m, n, tm, tn)
    return pl.pallas_call(
        functools.partial(_mm_ab_kernel, tn=tn),
        grid=(m // tm, n // tn),
        in_specs=[pl.BlockSpec((tm, k), lambda i, j: (i, 0)), _ssq_spec(tm), _w_spec(w, k, tn),
                  pl.BlockSpec((tm, HEAD_DIM), lambda i, j: (i, 0)), pl.BlockSpec((tm, HEAD_DIM), lambda i, j: (i, 0))],
        out_specs=pl.BlockSpec((tm, tn), lambda i, j: (i, j)),
        out_shape=jax.ShapeDtypeStruct((m, n), out_dtype),
        compiler_params=_params(("parallel", "arbitrary")),
        name="matmul_ab",
    )(a, ssq, w, cos, sin)


def _band_mask(rows, halo, n_band):
    r = lax.broadcasted_iota(jnp.int32, (rows, halo + SUB), 0)
    c = lax.broadcasted_iota(jnp.int32, (rows, halo + SUB), 1)
    rc = (r + halo) >> 6
    cc = c >> 6
    return (cc <= rc) & (cc > rc - n_band)


def _bias_prep_kernel(tab_ref, o_ref):
    h = pl.program_id(0)
    width = B_REACH + SUB
    ulen = width + SUB
    n = lax.broadcasted_iota(jnp.int32, (8, ulen), 1)
    idx = jnp.clip(B_REACH + SUB - 1 - n, -REL_CLIP, REL_CLIP) + REL_CLIP

    def body(m, u):
        return jnp.where(idx == m, tab_ref[h, m], u)

    u = lax.fori_loop(0, N_REL, body, jnp.zeros((8, ulen), F32))
    x = jnp.broadcast_to(u[0:1, :], (SUB, ulen))
    x = pltpu.roll(x, ulen - (SUB - 1), axis=1, stride=1, stride_axis=0)
    o_ref[0] = jnp.where(_band_mask(SUB, B_REACH, B_BAND_CHUNKS), x[:, :width], NEG_INF)


def bias_prep(table):
    return pl.pallas_call(
        _bias_prep_kernel,
        grid=(HB,),
        in_specs=[pl.BlockSpec(memory_space=pltpu.SMEM)],
        out_specs=pl.BlockSpec((1, SUB, B_REACH + SUB), lambda h: (h, 0, 0)),
        out_shape=jax.ShapeDtypeStruct((HB, SUB, B_REACH + SUB), F32),
        compiler_params=_params(("arbitrary",)),
        name="bias_prep",
    )(table)


def _band_prompt_kernel(*refs, group, halo, n_band, has_bias, has_sink):
    q_ref, km_ref, kh_ref, vm_ref, vh_ref = refs[:5]
    rest = list(refs[5:])
    bias_ref = rest.pop(0) if has_bias else None
    sink_ref = rest.pop(0) if has_sink else None
    o_ref = rest.pop(0)
    h = pl.program_id(0)
    i = pl.program_id(1)
    qb = q_ref.shape[0]
    width = halo + SUB
    k = jnp.concatenate([kh_ref[...], km_ref[...]], axis=0)
    v = jnp.concatenate([vh_ref[...], vm_ref[...]], axis=0)
    c = lax.broadcasted_iota(jnp.int32, (SUB, width), 1)
    if has_bias:
        bias = bias_ref[0]
    else:
        bias = jnp.where(_band_mask(SUB, halo, n_band), 0.0, NEG_INF).astype(F32)
    for sb in range(qb // SUB):
        kw = k[sb * SUB:sb * SUB + width]
        vw = v[sb * SUB:sb * SUB + width]
        first_valid = halo - i * qb - sb * SUB
        for g in range(group):
            q = q_ref[sb * SUB:(sb + 1) * SUB, g * HEAD_DIM:(g + 1) * HEAD_DIM]
            s = _dot_t(q, kw) + bias
            s = jnp.where(c >= first_valid, s, NEG_INF)
            m = jnp.max(s, axis=-1, keepdims=True)
            if has_sink:
                sk = sink_ref[h * group + g]
                m = jnp.maximum(m, sk)
            p = jnp.exp(s - m)
            l = jnp.sum(p, axis=-1, keepdims=True)
            if has_sink:
                l = l + jnp.exp(sk - m)
            o = _dot(p.astype(BF16), vw) / l
            o_ref[sb * SUB:(sb + 1) * SUB, g * HEAD_DIM:(g + 1) * HEAD_DIM] = o.astype(o_ref.dtype)


def band_prompt(proj, *, q_col, k_col, v_col, n_kv, group, halo, n_band, bias=None, sink=None):
    s = proj.shape[0]
    qb = BAND_QB
    assert s % qb == 0 and qb % halo == 0 and halo % CHUNK == 0
    qw = group * HEAD_DIM
    hpb = qb // halo
    qc, kc, vc = q_col // qw, k_col // HEAD_DIM, v_col // HEAD_DIM
    main = lambda c0: pl.BlockSpec((qb, HEAD_DIM), lambda h, i: (i, c0 + h))
    halo_spec = lambda c0: pl.BlockSpec((halo, HEAD_DIM), lambda h, i: (jnp.maximum(i * hpb - 1, 0), c0 + h))
    in_specs = [pl.BlockSpec((qb, qw), lambda h, i: (i, qc + h)), main(kc), halo_spec(kc), main(vc), halo_spec(vc)]
    args = [proj, proj, proj, proj, proj]
    if bias is not None:
        in_specs.append(pl.BlockSpec((1, SUB, halo + SUB), lambda h, i: (h, 0, 0)))
        args.append(bias)
    if sink is not None:
        in_specs.append(pl.BlockSpec(memory_space=pltpu.SMEM))
        args.append(sink)
    return pl.pallas_call(
        functools.partial(_band_prompt_kernel, group=group, halo=halo, n_band=n_band,
                          has_bias=bias is not None, has_sink=sink is not None),
        grid=(n_kv, s // qb),
        in_specs=in_specs,
        out_specs=pl.BlockSpec((qb, qw), lambda h, i: (i, h)),
        out_shape=jax.ShapeDtypeStruct((s, n_kv * qw), BF16),
        compiler_params=_params(("parallel", "arbitrary")),
        name="band_prompt",
    )(*args)


def _band_sample_kernel(*refs, group, n_kv, has_bias, has_sink):
    q_ref, kn_ref, vn_ref, kc_ref, vc_ref = refs[:5]
    rest = list(refs[5:])
    bias_ref = rest.pop(0) if has_bias else None
    sink_ref = rest.pop(0) if has_sink else None
    o_ref, ko_ref, vo_ref = rest[-3:]
    t = q_ref.shape[0]
    lc = kc_ref.shape[0]
    ko_ref[:lc - t] = kc_ref[t:]
    vo_ref[:lc - t] = vc_ref[t:]
    if has_sink:
        row = lax.broadcasted_iota(jnp.int32, (group * t, 1), 0)
    for h in range(n_kv):
        cols = slice(h * HEAD_DIM, (h + 1) * HEAD_DIM)
        kn = kn_ref[:, cols]
        vn = vn_ref[:, cols]
        ko_ref[lc - t:, h, :] = kn
        vo_ref[lc - t:, h, :] = vn
        kc = kc_ref[:, h, :].astype(BF16)
        vc = vc_ref[:, h, :].astype(BF16)
        heads = [h * group + g for g in range(group)]
        q = jnp.concatenate([q_ref[:, j * HEAD_DIM:(j + 1) * HEAD_DIM] for j in heads], axis=0).astype(BF16)
        sc = _dot_t(q, kc)
        sn = _dot_t(q, kn.astype(BF16))
        if has_bias:
            sc = sc + bias_ref[h, :t, :lc]
            sn = sn + bias_ref[h, :t, lc:lc + t]
        m = jnp.maximum(jnp.max(sc, axis=-1, keepdims=True), jnp.max(sn, axis=-1, keepdims=True))
        if has_sink:
            sk = jnp.zeros((group * t, 1), F32)
            for g, j in enumerate(heads):
                sk = jnp.where((row >= g * t) & (row < (g + 1) * t), sink_ref[j], sk)
            m = jnp.maximum(m, sk)
        pc = jnp.exp(sc - m)
        pn = jnp.exp(sn - m)
        l = jnp.sum(pc, axis=-1, keepdims=True) + jnp.sum(pn, axis=-1, keepdims=True)
        if has_sink:
            l = l + jnp.exp(sk - m)
        o = (_dot(pc.astype(BF16), vc) + _dot(pn.astype(BF16), vn.astype(BF16))) / l
        for g, j in enumerate(heads):
            o_ref[:, j * HEAD_DIM:(j + 1) * HEAD_DIM] = o[g * t:(g + 1) * t].astype(o_ref.dtype)


def band_sample(q, k_new, v_new, cache_k, cache_v, stack_k, stack_v, layer, *, t, group, bias=None, sink=None):
    _, nb, lc, n_kv, _ = cache_k.shape
    rows = lambda width: pl.BlockSpec((t, width), lambda b: (b, 0))
    cache_spec = pl.BlockSpec((None, None, lc, n_kv, HEAD_DIM), lambda b: (layer, b, 0, 0, 0))
    in_specs = [rows(q.shape[1]), rows(k_new.shape[1]), rows(v_new.shape[1]), cache_spec, cache_spec]
    args = [q, k_new, v_new, cache_k, cache_v]
    if bias is not None:
        assert lc == B_REACH and t <= CHUNK
        in_specs.append(pl.BlockSpec(bias.shape, lambda b: (0, 0, 0)))
        args.append(bias)
    if sink is not None:
        in_specs.append(pl.BlockSpec(memory_space=pltpu.SMEM))
        args.append(sink)
    aliases = {}
    if stack_k is not None:
        aliases = {len(args): 1, len(args) + 1: 2}
        in_specs += [pl.BlockSpec(memory_space=pl.ANY)] * 2
        args += [stack_k, stack_v]
    return pl.pallas_call(
        functools.partial(_band_sample_kernel, group=group, n_kv=n_kv, has_bias=bias is not None,
                          has_sink=sink is not None),
        grid=(nb,),
        in_specs=in_specs,
        out_specs=[rows(q.shape[1]), cache_spec, cache_spec],
        out_shape=[jax.ShapeDtypeStruct(q.shape, BF16),
                   jax.ShapeDtypeStruct(cache_k.shape, F32), jax.ShapeDtypeStruct(cache_v.shape, F32)],
        input_output_aliases=aliases,
        compiler_params=_params(("parallel",)),
        name="band_sample",
    )(*args)


def _tri2():
    j = lax.broadcasted_iota(jnp.int32, (SB_BLOCK, 2 * SB_BLOCK), 0)
    s = lax.broadcasted_iota(jnp.int32, (SB_BLOCK, 2 * SB_BLOCK), 1)
    return jnp.where((s >= SB_BLOCK) | (j > s), 1.0, 0.0).astype(BF16)


def _softplus(z):
    neg_abs = lax.bitcast_convert_type(lax.bitcast_convert_type(z, jnp.int32) | jnp.int32(-2 ** 31), F32)
    return jnp.maximum(z, 0.0) + jnp.log(1.0 + jnp.exp(neg_abs))


def _sb_tile(z, tri2, carry):
    nls = _softplus(z)
    la = _dot(nls.astype(BF16), tri2)
    logw = z - nls - la[:, :SB_BLOCK]
    if carry is None:
        return logw, la[:, SB_BLOCK:]
    return logw - carry, carry + la[:, SB_BLOCK:]


def _causal_tile_mask(rows):
    r = lax.broadcasted_iota(jnp.int32, (rows, SB_BLOCK), 0)
    c = lax.broadcasted_iota(jnp.int32, (rows, SB_BLOCK), 1)
    return c < r


def _sb_block(ref, j):
    return ref[pl.ds(pl.multiple_of(j * SB_BLOCK, SB_BLOCK), SB_BLOCK), :].astype(BF16)


def _sb_older_blocks(q, k_ref, v_ref, tri2, first, acc, carry):
    def cond(state):
        j, _, carry = state
        return (j >= 0) & (jnp.min(carry) < SB_SKIP_LOG)

    def body(state):
        j, acc, carry = state
        logw, carry = _sb_tile(_dot_t(q, _sb_block(k_ref, j)), tri2, carry)
        acc = acc + _dot(jnp.exp(logw).astype(BF16), _sb_block(v_ref, j))
        return j - 1, acc, carry

    _, acc, _ = lax.while_loop(cond, body, (first, acc, carry))
    return acc


def _sb_prompt_kernel(q_ref, k_ref, v_ref, o_ref):
    tq = q_ref.shape[0]
    nsub = tq // SB_BLOCK
    nwin = nsub + SB_LOOKBACK
    pid = pl.program_id(1)
    base = pid * nsub
    is_first = pid == 0
    start = pl.multiple_of(jnp.maximum(base - SB_LOOKBACK, 0) * SB_BLOCK, SB_BLOCK)
    q = q_ref[...]
    kw = k_ref[pl.ds(start, nwin * SB_BLOCK), :]
    vw = v_ref[pl.ds(start, nwin * SB_BLOCK), :]
    kw = jnp.where(is_first, jnp.roll(kw, SB_LOOKBACK * SB_BLOCK, axis=0), kw)
    vw = jnp.where(is_first, jnp.roll(vw, SB_LOOKBACK * SB_BLOCK, axis=0), vw)
    tri2 = _tri2()
    z = _dot_t(q, kw)
    diag = _causal_tile_mask(SB_BLOCK)
    accs, carries = [], []
    for a in range(nsub):
        rows = slice(a * SB_BLOCK, (a + 1) * SB_BLOCK)
        carry = None
        pieces = []
        for b in range(a + SB_LOOKBACK, -1, -1):
            zt = z[rows, b * SB_BLOCK:(b + 1) * SB_BLOCK]
            if b == a + SB_LOOKBACK:
                zt = jnp.where(diag, zt, NEG_INF)
            if b < SB_LOOKBACK:
                zt = jnp.where(is_first, NEG_INF, zt)
            logw, carry = _sb_tile(zt, tri2, carry)
            pieces.insert(0, logw)
        w = jnp.exp(jnp.concatenate(pieces, axis=1)).astype(BF16)
        accs.append(_dot(w, vw[:(a + SB_LOOKBACK + 1) * SB_BLOCK]))
        carries.append(carry)
    acc = jnp.concatenate(accs, axis=0)
    carry = jnp.concatenate(carries, axis=0)
    acc = _sb_older_blocks(q, k_ref, v_ref, tri2, base - SB_LOOKBACK - 1, acc, carry)
    o_ref[...] = acc.astype(o_ref.dtype)


def sb_prompt(q, k, v):
    s = q.shape[0]
    tq = min(SB_QUERY_ROWS, s)
    assert s % tq == 0 and tq % SB_BLOCK == 0 and s >= tq + SB_LOOKBACK * SB_BLOCK
    head_all = pl.BlockSpec((s, HEAD_DIM), lambda h, i: (0, h))
    return pl.pallas_call(
        _sb_prompt_kernel,
        grid=(HC, s // tq),
        in_specs=[pl.BlockSpec((tq, HEAD_DIM), lambda h, i: (i, h)), head_all, head_all],
        out_specs=pl.BlockSpec((tq, HEAD_DIM), lambda h, i: (i, h)),
        out_shape=jax.ShapeDtypeStruct(q.shape, BF16),
        compiler_params=_params(("parallel", "arbitrary")),
        name="sb_prompt",
    )(q, k, v)


def _sb_sample_kernel(q_ref, kn_ref, vn_ref, kc_ref, vc_ref, o_ref, acc_ref, carry_ref):
    s = pl.program_id(1)
    t = q_ref.shape[0]
    n_heads = acc_ref.shape[0]
    tri2 = _tri2()

    def head_q(h):
        return q_ref[:, h * HEAD_DIM:(h + 1) * HEAD_DIM].astype(BF16)

    @pl.when(s == 0)
    def _():
        zeros = jnp.zeros((SB_BLOCK - t, HEAD_DIM), BF16)
        mask = _causal_tile_mask(t)
        for h in range(n_heads):
            cols = slice(h * HEAD_DIM, (h + 1) * HEAD_DIM)
            kb = jnp.concatenate([kn_ref[:, cols].astype(BF16), zeros], axis=0)
            vb = jnp.concatenate([vn_ref[:, cols].astype(BF16), zeros], axis=0)
            logw, carry = _sb_tile(jnp.where(mask, _dot_t(head_q(h), kb), NEG_INF), tri2, None)
            acc_ref[h] = _dot(jnp.exp(logw).astype(BF16), vb)
            carry_ref[h] = carry

    @pl.when(jnp.min(carry_ref[...]) < SB_SKIP_LOG)
    def _():
        for h in range(n_heads):
            kb = kc_ref[:, h, :].astype(BF16)
            vb = vc_ref[:, h, :].astype(BF16)
            logw, carry = _sb_tile(_dot_t(head_q(h), kb), tri2, carry_ref[h])
            acc_ref[h] += _dot(jnp.exp(logw).astype(BF16), vb)
            carry_ref[h] = carry

    @pl.when(s == pl.num_programs(1) - 1)
    def _():
        for h in range(n_heads):
            o_ref[:, h * HEAD_DIM:(h + 1) * HEAD_DIM] = acc_ref[h].astype(o_ref.dtype)


def sb_sample(q, k_new, v_new, cache_k, cache_v, layer, *, t):
    _, nb, past, n_heads, _ = cache_k.shape
    assert past % SB_BLOCK == 0 and t <= SB_BLOCK
    nblk = past // SB_BLOCK
    rows = pl.BlockSpec((t, n_heads * HEAD_DIM), lambda b, s: (b, 0))
    cache_spec = pl.BlockSpec((None, None, SB_BLOCK, n_heads, HEAD_DIM), lambda b, s: (layer, b, nblk - 1 - s, 0, 0))
    return pl.pallas_call(
        _sb_sample_kernel,
        grid=(nb, nblk),
        in_specs=[rows, rows, rows, cache_spec, cache_spec],
        out_specs=rows,
        out_shape=jax.ShapeDtypeStruct(q.shape, BF16),
        scratch_shapes=[pltpu.VMEM((n_heads, t, HEAD_DIM), F32), pltpu.VMEM((n_heads, t, HEAD_DIM), F32)],
        compiler_params=_params(("parallel", "arbitrary")),
        name="sb_sample",
    )(q, k_new, v_new, cache_k, cache_v)


def _rope_table_kernel(inv_ref, cos_ref, sin_ref, *, pos0):
    tm = cos_ref.shape[0]
    pos = lax.broadcasted_iota(jnp.int32, (tm, HEAD_DIM), 0) + (pos0 + pl.program_id(0) * tm)
    lane = lax.broadcasted_iota(jnp.int32, (tm, HEAD_DIM), 1)
    ang = pos.astype(F32) * inv_ref[...]
    sin = jnp.sin(ang)
    cos_ref[...] = jnp.cos(ang)
    sin_ref[...] = jnp.where(lane < HEAD_DIM // 2, -sin, sin)


def _rope_tables(n, pos0):
    half = HEAD_DIM // 2
    inv = ROPE_THETA ** (-jnp.arange(half, dtype=F32) / half)
    inv = jnp.concatenate([inv, inv]).reshape(1, HEAD_DIM)
    tm = min(1024, n)
    assert n % tm == 0
    return pl.pallas_call(
        functools.partial(_rope_table_kernel, pos0=pos0),
        grid=(n // tm,),
        in_specs=[pl.BlockSpec((1, HEAD_DIM), lambda i: (0, 0))],
        out_specs=[pl.BlockSpec((tm, HEAD_DIM), lambda i: (i, 0))] * 2,
        out_shape=[jax.ShapeDtypeStruct((n, HEAD_DIM), F32)] * 2,
        compiler_params=_params(("parallel",)),
        name="rope_tables",
    )(inv)


def kernel(x_prompt, x_sample, cache_a_k, cache_a_v, cache_b_k, cache_b_v, cache_c_k, cache_c_v, norm_mix, w_in_ab, sink_a, rel_bias_b, w_out_ab, w_in_c, w_out_c, norm_ffn, w_gate, w_up, w_down, norm_final):
    bp, seq, d = x_prompt.shape
    nb, t, _ = x_sample.shape
    depth = norm_mix.shape[0]
    past = cache_c_k.shape[2]
    assert bp == 1 and seq >= B_REACH
    keep_a, keep_b = min(A_REACH, seq), min(B_REACH, seq)
    wc = HC * HEAD_DIM

    yp = x_prompt.reshape(seq, d)
    ys = x_sample.reshape(nb * t, d)
    cos_p, sin_p = _rope_tables(seq, 0)
    cos_s, sin_s = _rope_tables(t, past)
    cos_s, sin_s = jnp.tile(cos_s, (nb, 1)), jnp.tile(sin_s, (nb, 1))

    outs = {name: [] for name in ("pa_k", "pa_v", "pb_k", "pb_v", "sc_k", "sc_v")}
    n_c = cache_c_k.shape[0]
    sa_k = sa_v = sb_k = sb_v = pc_k = pc_v = None
    hp, qp = norm_inputs(yp)
    hs, qs_ = norm_inputs(ys)
    for layer in range(depth):
        i = layer // 2
        if layer % 2 == 0:
            w_in = cast_bf16(w_in_ab, i, norm_mix[layer], col_tile=AB_TILE)
            w_out = cast_bf16(w_out_ab, i, col_tile=OUT_TILE)
            bias = bias_prep(rel_bias_b[i])
            proj = matmul_ab(hp, qp, w_in, cos_p, sin_p, BF16)
            oa = band_prompt(proj, q_col=0, k_col=AB_SPLITS[0], v_col=AB_SPLITS[1], n_kv=KV_A, group=G_A,
                             halo=A_REACH, n_band=A_BAND_CHUNKS, sink=sink_a[i])
            ob = band_prompt(proj, q_col=AB_SPLITS[2], k_col=AB_SPLITS[3], v_col=AB_SPLITS[4], n_kv=HB, group=1,
                             halo=B_REACH, n_band=B_BAND_CHUNKS, bias=bias)
            tail = matmul_ab(hp[seq - keep_b:], qp[seq - keep_b:], w_in, cos_p[seq - keep_b:], sin_p[seq - keep_b:], F32)
            yp, hp, qp = matmul2_res(oa, ob, w_out, yp)
            outs["pa_k"].append(tail[keep_b - keep_a:, AB_SPLITS[0]:AB_SPLITS[1]].reshape(1, keep_a, KV_A, HEAD_DIM))
            outs["pa_v"].append(tail[keep_b - keep_a:, AB_SPLITS[1]:AB_SPLITS[2]].reshape(1, keep_a, KV_A, HEAD_DIM))
            outs["pb_k"].append(tail[:, AB_SPLITS[3]:AB_SPLITS[4]].reshape(1, keep_b, HB, HEAD_DIM))
            outs["pb_v"].append(tail[:, AB_SPLITS[4]:].reshape(1, keep_b, HB, HEAD_DIM))
            projs = matmul_ab(hs, qs_, w_in, cos_s, sin_s, F32)
            qa, ka, va, qb, kb, vb = jnp.split(projs, AB_SPLITS, axis=1)
            oa, sa_k, sa_v = band_sample(qa, ka, va, cache_a_k, cache_a_v, sa_k, sa_v, i, t=t, group=G_A,
                                         sink=sink_a[i])
            ob, sb_k, sb_v = band_sample(qb, kb, vb, cache_b_k, cache_b_v, sb_k, sb_v, i, t=t, group=1, bias=bias)
            ys, hs, qs_ = matmul2_res(oa, ob, w_out, ys)
        else:
            w_in = cast_bf16(w_in_c, i, norm_mix[layer])
            w_out = cast_bf16(w_out_c, i)
            q = matmul(hp, qp, w_in, BF16, scale=QK_SCALE, col0=0, n=wc)
            pc_k, k16 = matmul_heads(hp, qp, w_in, pc_k, i, n_c, col0=wc, n=wc)
            pc_v, v16 = matmul_heads(hp, qp, w_in, pc_v, i, n_c, col0=2 * wc, n=wc)
            att = sb_prompt(q, k16, v16)
            yp, hp, qp = matmul_res(att, w_out, yp, tm=1024, tn=512)
            q = matmul(hs, qs_, w_in, F32, scale=QK_SCALE, col0=0, n=wc)
            ks = matmul(hs, qs_, w_in, F32, col0=wc, n=wc)
            vs = matmul(hs, qs_, w_in, F32, col0=2 * wc, n=wc)
            att = sb_sample(q, ks, vs, cache_c_k, cache_c_v, i, t=t)
            ys, hs, qs_ = matmul_res(att, w_out, ys, tm=1024, tn=512)
            outs["sc_k"].append(ks.reshape(nb, t, HC, HEAD_DIM))
            outs["sc_v"].append(vs.reshape(nb, t, HC, HEAD_DIM))
        wg = cast_bf16(w_gate, layer, norm_ffn[layer], col_tile=FF_TILE)
        wu = cast_bf16(w_up, layer, norm_ffn[layer], col_tile=FF_TILE)
        wd = cast_bf16(w_down, layer, col_tile=OUT_TILE)
        yp, hp, qp = matmul_res(gateup(hp, qp, wg, wu), wd, yp)
        ys, hs, qs_ = matmul_res(gateup(hs, qs_, wg, wu), wd, ys)

    y_prompt = rmsnorm(yp, norm_final, F32).reshape(1, seq, d)
    y_sample = rmsnorm(ys, norm_final, F32).reshape(nb, t, d)
    st = {name: jnp.stack(v) for name, v in outs.items()}
    pc_k = pc_k.reshape(n_c, 1, seq, HC, HEAD_DIM)
    pc_v = pc_v.reshape(n_c, 1, seq, HC, HEAD_DIM)
    return (y_prompt, y_sample, st["pa_k"], st["pa_v"], st["pb_k"], st["pb_v"], pc_k, pc_v,
            sa_k, sa_v, sb_k, sb_v, st["sc_k"], st["sc_v"])
```

```python
import functools
from typing import NamedTuple, Optional

import jax
import jax.numpy as jnp
from jax import lax
from jax.experimental import pallas as pl
from jax.experimental.pallas import tpu as pltpu

HEAD_DIM = 128
CHUNK = 64
HA = 16
KV_A = 4
G_A = HA // KV_A
HB = 16
HC = 16
A_BAND_CHUNKS = 3
B_BAND_CHUNKS = 9
A_REACH = (A_BAND_CHUNKS - 1) * CHUNK
B_REACH = (B_BAND_CHUNKS - 1) * CHUNK
REL_CLIP = 256
N_REL = 2 * REL_CLIP + 1
SB_BLOCK = 128
ROPE_THETA = 10000.0
RMS_EPS = 1e-6
NEG_INF = -1e30
QA_W = HA * HEAD_DIM
KA_W = KV_A * HEAD_DIM
QB_W = HB * HEAD_DIM
AB_SPLITS = (QA_W, QA_W + KA_W, QA_W + 2 * KA_W, QA_W + 2 * KA_W + QB_W, QA_W + 2 * KA_W + 2 * QB_W)
QK_SCALE = HEAD_DIM ** -0.5

SB_QUERY_ROWS = 512
SB_LOOKBACK = 2
SB_SKIP_LOG = 106.0
BF16_ROWS = 16
AB_TILE = 512
OUT_TILE = 512
FF_TILE = 256
SUB = 128
BAND_QB = 512
VMEM_LIMIT_MB = 56

F32 = jnp.float32
BF16 = jnp.bfloat16


def _params(semantics):
    return pltpu.CompilerParams(dimension_semantics=semantics, vmem_limit_bytes=VMEM_LIMIT_MB << 20)


def _dot(a, b):
    return jnp.dot(a, b, preferred_element_type=F32)


def _dot_t(a, b):
    return lax.dot_general(a, b, (((1,), (1,)), ((), ())), preferred_element_type=F32)


def _rmsnorm_kernel(x_ref, g_ref, o_ref):
    x = x_ref[...]
    ms = jnp.mean(x * x, axis=-1, keepdims=True)
    o_ref[...] = (x * lax.rsqrt(ms + RMS_EPS) * g_ref[...]).astype(o_ref.dtype)


def rmsnorm(x, g, out_dtype):
    m, d = x.shape
    tm = min(256, m)
    return pl.pallas_call(
        _rmsnorm_kernel,
        grid=(m // tm,),
        in_specs=[pl.BlockSpec((tm, d), lambda i: (i, 0)), pl.BlockSpec((1, d), lambda i: (0, 0))],
        out_specs=pl.BlockSpec((tm, d), lambda i: (i, 0)),
        out_shape=jax.ShapeDtypeStruct((m, d), out_dtype),
        compiler_params=_params(("parallel",)),
        name="rmsnorm",
    )(x, g.reshape(1, d))


def _lane_partial_ssq(y):
    sq = y * y
    part = sq[:, :HEAD_DIM]
    for c in range(1, y.shape[1] // HEAD_DIM):
        part = part + sq[:, c * HEAD_DIM:(c + 1) * HEAD_DIM]
    return part


def _row_rstd(ssq_ref, d):
    return lax.rsqrt(jnp.sum(ssq_ref[...], axis=-1, keepdims=True) / d + RMS_EPS)


def _norm_inputs_kernel(x_ref, o16_ref, ssq_ref):
    x = x_ref[...]
    o16_ref[...] = x.astype(o16_ref.dtype)
    ssq_ref[...] = _lane_partial_ssq(x)


def norm_inputs(x):
    m, d = x.shape
    tm = min(256, m)
    return pl.pallas_call(
        _norm_inputs_kernel,
        grid=(m // tm,),
        in_specs=[pl.BlockSpec((tm, d), lambda i: (i, 0))],
        out_specs=[pl.BlockSpec((tm, d), lambda i: (i, 0)), pl.BlockSpec((tm, HEAD_DIM), lambda i: (i, 0))],
        out_shape=[jax.ShapeDtypeStruct((m, d), BF16), jax.ShapeDtypeStruct((m, HEAD_DIM), F32)],
        compiler_params=_params(("parallel",)),
        name="norm_inputs",
    )(x)


def _cast_block(x_ref, gain_ref, o_ref):
    x = x_ref[...]
    if gain_ref is not None:
        x = x * gain_ref[...]
    x = x.astype(o_ref.dtype)
    if len(o_ref.shape) == 2:
        o_ref[...] = x
    else:
        ct = o_ref.shape[2]
        for c in range(o_ref.shape[0]):
            o_ref[c] = x[:, c * ct:(c + 1) * ct]


def _cast_kernel(x_ref, *rest):
    _cast_block(x_ref, rest[0] if len(rest) == 2 else None, rest[-1])


class CastJob(NamedTuple):
    stack: jax.Array
    layer: int
    gain: Optional[jax.Array]
    col_tile: Optional[int]


def _cast_job_specs(job, n_steps, step_of):
    _, k, n = job.stack.shape
    tr = next(r for r in range(BF16_ROWS, k + 1, BF16_ROWS) if k % r == 0 and k // r <= n_steps)
    last = k // tr - 1
    blk = lambda i, j: jnp.minimum(step_of(i, j), last)
    layer = job.layer
    in_specs = [pl.BlockSpec((None, tr, n), lambda i, j: (layer, blk(i, j), 0))]
    args = [job.stack]
    if job.gain is not None:
        in_specs.append(pl.BlockSpec((tr, 1), lambda i, j: (blk(i, j), 0)))
        args.append(job.gain.reshape(k, 1))
    if job.col_tile is None:
        out_spec = pl.BlockSpec((tr, n), lambda i, j: (blk(i, j), 0))
        out_shape = jax.ShapeDtypeStruct((k, n), BF16)
    else:
        ct = job.col_tile
        out_spec = pl.BlockSpec((n // ct, tr, ct), lambda i, j: (0, blk(i, j), 0))
        out_shape = jax.ShapeDtypeStruct((n // ct, k, ct), BF16)
    return in_specs, args, out_spec, out_shape, last + 1


def cast_bf16(w_stack, layer, gain=None, col_tile=None):
    _, k, n = w_stack.shape
    tr = min(256, k)
    assert k % tr == 0
    in_specs = [pl.BlockSpec((None, tr, n), lambda i: (layer, i, 0))]
    args = [w_stack]
    if gain is not None:
        in_specs.append(pl.BlockSpec((tr, 1), lambda i: (i, 0)))
        args.append(gain.reshape(k, 1))
    if col_tile is None:
        out_spec = pl.BlockSpec((tr, n), lambda i: (i, 0))
        out_shape = jax.ShapeDtypeStruct((k, n), BF16)
    else:
        assert n % col_tile == 0
        out_spec = pl.BlockSpec((n // col_tile, tr, col_tile), lambda i: (0, i, 0))
        out_shape = jax.ShapeDtypeStruct((n // col_tile, k, col_tile), BF16)
    return pl.pallas_call(
        _cast_kernel,
        grid=(k // tr,),
        in_specs=in_specs,
        out_specs=out_spec,
        out_shape=out_shape,
        compiler_params=_params(("parallel",)),
        name="cast_bf16",
    )(*args)


def _w_spec(w, k, tn, c0=0):
    if w.ndim == 2:
        return pl.BlockSpec((k, tn), lambda i, j: (0, c0 + j))
    assert w.shape[1:] == (k, tn), (w.shape, k, tn)
    return pl.BlockSpec((None, k, tn), lambda i, j: (c0 + j, 0, 0))


def _w_cols(w):
    return w.shape[1] if w.ndim == 2 else w.shape[0] * w.shape[2]


def _mm_kernel(a_ref, ssq_ref, w_ref, o_ref, *, scale):
    acc = _dot(a_ref[...], w_ref[...]) * (_row_rstd(ssq_ref, a_ref.shape[1]) * scale)
    o_ref[...] = acc.astype(o_ref.dtype)


def _mm_heads_kernel(a_ref, ssq_ref, w_ref, *rest):
    o32_ref, o16_ref = rest[-2:]
    acc = _dot(a_ref[...], w_ref[...]) * _row_rstd(ssq_ref, a_ref.shape[1])
    for h in range(o32_ref.shape[1]):
        o32_ref[:, h, :] = acc[:, h * HEAD_DIM:(h + 1) * HEAD_DIM]
    o16_ref[...] = acc.astype(o16_ref.dtype)


def _emit_residual(y, o_ref, o16_ref, ssq_ref):
    o_ref[...] = y
    o16_ref[...] = y.astype(o16_ref.dtype)
    part = _lane_partial_ssq(y)
    first = pl.program_id(1) == 0

    @pl.when(first)
    def _():
        ssq_ref[...] = part

    @pl.when(jnp.logical_not(first))
    def _():
        ssq_ref[...] += part


def _mm_res_kernel(a_ref, w_ref, r_ref, o_ref, o16_ref, ssq_ref):
    _emit_residual(r_ref[...] + _dot(a_ref[...], w_ref[...]), o_ref, o16_ref, ssq_ref)


def _mm2_res_kernel(a1_ref, a2_ref, w_ref, r_ref, o_ref, o16_ref, ssq_ref):
    k1 = a1_ref.shape[1]
    acc = _dot(a1_ref[...], w_ref[:k1, :]) + _dot(a2_ref[...], w_ref[k1:, :])
    _emit_residual(r_ref[...] + acc, o_ref, o16_ref, ssq_ref)


def _gateup_kernel(a_ref, ssq_ref, wg_ref, wu_ref, *rest, job_has_gain, job_blocks):
    n_jobs = len(job_has_gain)
    o_ref = rest[len(rest) - n_jobs - 1]
    a = a_ref[...]
    rstd = _row_rstd(ssq_ref, a_ref.shape[1])
    g = _dot(a, wg_ref[...]) * rstd
    u = _dot(a, wu_ref[...]) * rstd
    o_ref[...] = (g / (1.0 + jnp.exp(-g)) * u).astype(o_ref.dtype)
    step = pl.program_id(0) * pl.num_programs(1) + pl.program_id(1)
    pos = 0
    for job, (has_gain, n_blocks) in enumerate(zip(job_has_gain, job_blocks)):
        x_ref = rest[pos]
        gain_ref = rest[pos + 1] if has_gain else None
        pos += 2 if has_gain else 1
        pl.when(step < n_blocks)(functools.partial(_cast_block, x_ref, gain_ref, rest[len(rest) - n_jobs + job]))


def _mm_ab_kernel(a_ref, ssq_ref, w_ref, cos_ref, sin_ref, o_ref, *, tn):
    col0 = pl.program_id(1) * tn
    acc = _dot(a_ref[...], w_ref[...]) * _row_rstd(ssq_ref, a_ref.shape[1])
    is_rope = col0 < AB_SPLITS[1]
    is_q = (col0 < AB_SPLITS[0]) | ((col0 >= AB_SPLITS[2]) & (col0 < AB_SPLITS[3]))
    s = jnp.where(is_q, QK_SCALE, 1.0).astype(F32)

    @pl.when(is_rope)
    def _():
        cos = cos_ref[...]
        sin = sin_ref[...]
        for g in range(tn // HEAD_DIM):
            blk = acc[:, g * HEAD_DIM:(g + 1) * HEAD_DIM]
            rot = blk * cos + pltpu.roll(blk, HEAD_DIM // 2, axis=1) * sin
            o_ref[:, g * HEAD_DIM:(g + 1) * HEAD_DIM] = (rot * s).astype(o_ref.dtype)

    @pl.when(jnp.logical_not(is_rope))
    def _():
        o_ref[...] = (acc * s).astype(o_ref.dtype)


def _tiles(m, n, tm, tn):
    tm = min(tm, m)
    tn = min(tn, n)
    assert m % tm == 0 and n % tn == 0, (m, n, tm, tn)
    return tm, tn


def _ssq_spec(tm):
    return pl.BlockSpec((tm, HEAD_DIM), lambda i, j: (i, 0))


def _residual_out_specs(tm, tn):
    return [pl.BlockSpec((tm, tn), lambda i, j: (i, j)), pl.BlockSpec((tm, tn), lambda i, j: (i, j)), _ssq_spec(tm)]


def _residual_out_shapes(m, n):
    return [jax.ShapeDtypeStruct((m, n), F32), jax.ShapeDtypeStruct((m, n), BF16),
            jax.ShapeDtypeStruct((m, HEAD_DIM), F32)]


def matmul(a, ssq, w, out_dtype, *, scale=1.0, col0=0, n=None, tm=1024, tn=512):
    m, k = a.shape
    n = w.shape[1] - col0 if n is None else n
    tm, tn = _tiles(m, n, tm, tn)
    assert col0 % tn == 0
    c0 = col0 // tn
    return pl.pallas_call(
        functools.partial(_mm_kernel, scale=scale),
        grid=(m // tm, n // tn),
        in_specs=[pl.BlockSpec((tm, k), lambda i, j: (i, 0)), _ssq_spec(tm),
                  pl.BlockSpec((k, tn), lambda i, j: (0, c0 + j))],
        out_specs=pl.BlockSpec((tm, tn), lambda i, j: (i, j)),
        out_shape=jax.ShapeDtypeStruct((m, n), out_dtype),
        compiler_params=_params(("parallel", "arbitrary")),
        name="matmul",
    )(a, ssq, w)


def matmul_heads(a, ssq, w, stack, layer, n_layers, *, col0, n, tm=1024, tn=1024):
    m, k = a.shape
    tm, tn = _tiles(m, n, tm, tn)
    assert col0 % tn == 0 and tn % (8 * HEAD_DIM) == 0
    c0 = col0 // tn
    hpt = tn // HEAD_DIM
    in_specs = [pl.BlockSpec((tm, k), lambda i, j: (i, 0)), _ssq_spec(tm),
                pl.BlockSpec((k, tn), lambda i, j: (0, c0 + j))]
    args = [a, ssq, w]
    if stack is not None:
        in_specs.append(pl.BlockSpec(memory_space=pl.ANY))
        args.append(stack)
    return pl.pallas_call(
        _mm_heads_kernel,
        grid=(m // tm, n // tn),
        in_specs=in_specs,
        out_specs=[pl.BlockSpec((None, tm, hpt, HEAD_DIM), lambda i, j: (layer, i, j, 0)),
                   pl.BlockSpec((tm, tn), lambda i, j: (i, j))],
        out_shape=[jax.ShapeDtypeStruct((n_layers, m, n // HEAD_DIM, HEAD_DIM), F32), jax.ShapeDtypeStruct((m, n), BF16)],
        input_output_aliases={} if stack is None else {3: 0},
        compiler_params=_params(("parallel", "arbitrary")),
        name="matmul_heads",
    )(*args)


def matmul_res(a, w, res, *, tm=512, tn=OUT_TILE):
    m, k = a.shape
    n = _w_cols(w)
    tm, tn = _tiles(m, n, tm, tn)
    return pl.pallas_call(
        _mm_res_kernel,
        grid=(m // tm, n // tn),
        in_specs=[pl.BlockSpec((tm, k), lambda i, j: (i, 0)), _w_spec(w, k, tn),
                  pl.BlockSpec((tm, tn), lambda i, j: (i, j))],
        out_specs=_residual_out_specs(tm, tn),
        out_shape=_residual_out_shapes(m, n),
        compiler_params=_params(("parallel", "arbitrary")),
        name="matmul_res",
    )(a, w, res)


def matmul2_res(a1, a2, w, res, *, tm=1024, tn=OUT_TILE):
    m, k1 = a1.shape
    k2 = a2.shape[1]
    n = _w_cols(w)
    tm, tn = _tiles(m, n, tm, tn)
    return pl.pallas_call(
        _mm2_res_kernel,
        grid=(m // tm, n // tn),
        in_specs=[pl.BlockSpec((tm, k1), lambda i, j: (i, 0)), pl.BlockSpec((tm, k2), lambda i, j: (i, 0)),
                  _w_spec(w, k1 + k2, tn), pl.BlockSpec((tm, tn), lambda i, j: (i, j))],
        out_specs=_residual_out_specs(tm, tn),
        out_shape=_residual_out_shapes(m, n),
        compiler_params=_params(("parallel", "arbitrary")),
        name="matmul2_res",
    )(a1, a2, w, res)


def gateup(a, ssq, wg, wu, cast_jobs=(), *, tm=1024, tn=FF_TILE):
    m, k = a.shape
    n = _w_cols(wg)
    tm, tn = _tiles(m, n, tm, tn)
    grid = (m // tm, n // tn)
    in_specs = [pl.BlockSpec((tm, k), lambda i, j: (i, 0)), _ssq_spec(tm), _w_spec(wg, k, tn), _w_spec(wu, k, tn)]
    args = [a, ssq, wg, wu]
    out_specs = [pl.BlockSpec((tm, tn), lambda i, j: (i, j))]
    out_shape = [jax.ShapeDtypeStruct((m, n), BF16)]
    job_blocks = []
    for job in cast_jobs:
        j_in, j_args, j_out, j_shape, n_blocks = _cast_job_specs(job, grid[0] * grid[1], lambda i, j: i * grid[1] + j)
        in_specs += j_in
        args += j_args
        out_specs.append(j_out)
        out_shape.append(j_shape)
        job_blocks.append(n_blocks)
    outs = pl.pallas_call(
        functools.partial(_gateup_kernel, job_has_gain=tuple(job.gain is not None for job in cast_jobs),
                          job_blocks=tuple(job_blocks)),
        grid=grid,
        in_specs=in_specs,
        out_specs=out_specs,
        out_shape=out_shape,
        compiler_params=_params(("parallel", "arbitrary")),
        name="gateup",
    )(*args)
    return outs[0], list(outs[1:])


def matmul_ab(a, ssq, w, cos, sin, out_dtype, *, tm=2048):
    m, k = a.shape
    n = _w_cols(w)
    tn = AB_TILE
    assert all(s % tn == 0 for s in AB_SPLITS)
    tm, tn = _tiles(m, n, tm, tn)
    return pl.pallas_call(
        functools.partial(_mm_ab_kernel, tn=tn),
        grid=(m // tm, n // tn),
        in_specs=[pl.BlockSpec((tm, k), lambda i, j: (i, 0)), _ssq_spec(tm), _w_spec(w, k, tn),
                  pl.BlockSpec((tm, HEAD_DIM), lambda i, j: (i, 0)), pl.BlockSpec((tm, HEAD_DIM), lambda i, j: (i, 0))],
        out_specs=pl.BlockSpec((tm, tn), lambda i, j: (i, j)),
        out_shape=jax.ShapeDtypeStruct((m, n), out_dtype),
        compiler_params=_params(("parallel", "arbitrary")),
        name="matmul_ab",
    )(a, ssq, w, cos, sin)


def _band_mask(rows, halo, n_band):
    r = lax.broadcasted_iota(jnp.int32, (rows, halo + SUB), 0)
    c = lax.broadcasted_iota(jnp.int32, (rows, halo + SUB), 1)
    rc = (r + halo) >> 6
    cc = c >> 6
    return (cc <= rc) & (cc > rc - n_band)


def _bias_prep_kernel(tab_ref, o_ref):
    h = pl.program_id(0)
    width = B_REACH + SUB
    ulen = width + SUB
    n = lax.broadcasted_iota(jnp.int32, (8, ulen), 1)
    idx = jnp.clip(B_REACH + SUB - 1 - n, -REL_CLIP, REL_CLIP) + REL_CLIP

    def body(m, u):
        return jnp.where(idx == m, tab_ref[h, m], u)

    u = lax.fori_loop(0, N_REL, body, jnp.zeros((8, ulen), F32))
    x = jnp.broadcast_to(u[0:1, :], (SUB, ulen))
    x = pltpu.roll(x, ulen - (SUB - 1), axis=1, stride=1, stride_axis=0)
    o_ref[0] = jnp.where(_band_mask(SUB, B_REACH, B_BAND_CHUNKS), x[:, :width], NEG_INF)


def bias_prep(table):
    return pl.pallas_call(
        _bias_prep_kernel,
        grid=(HB,),
        in_specs=[pl.BlockSpec(memory_space=pltpu.SMEM)],
        out_specs=pl.BlockSpec((1, SUB, B_REACH + SUB), lambda h: (h, 0, 0)),
        out_shape=jax.ShapeDtypeStruct((HB, SUB, B_REACH + SUB), F32),
        compiler_params=_params(("arbitrary",)),
        name="bias_prep",
    )(table)


def _band_prompt_kernel(*refs, group, halo, n_band, has_bias, has_sink):
    q_ref, km_ref, kh_ref, vm_ref, vh_ref = refs[:5]
    rest = list(refs[5:])
    bias_ref = rest.pop(0) if has_bias else None
    sink_ref = rest.pop(0) if has_sink else None
    o_ref = rest.pop(0)
    h = pl.program_id(0)
    i = pl.program_id(1)
    qb = q_ref.shape[0]
    width = halo + SUB
    k = jnp.concatenate([kh_ref[...], km_ref[...]], axis=0)
    v = jnp.concatenate([vh_ref[...], vm_ref[...]], axis=0)
    c = lax.broadcasted_iota(jnp.int32, (SUB, width), 1)
    if has_bias:
        bias = bias_ref[0]
    else:
        bias = jnp.where(_band_mask(SUB, halo, n_band), 0.0, NEG_INF).astype(F32)
    for sb in range(qb // SUB):
        kw = k[sb * SUB:sb * SUB + width]
        vw = v[sb * SUB:sb * SUB + width]
        first_valid = halo - i * qb - sb * SUB
        for g in range(group):
            q = q_ref[sb * SUB:(sb + 1) * SUB, g * HEAD_DIM:(g + 1) * HEAD_DIM]
            s = _dot_t(q, kw) + bias
            s = jnp.where(c >= first_valid, s, NEG_INF)
            m = jnp.max(s, axis=-1, keepdims=True)
            if has_sink:
                sk = sink_ref[h * group + g]
                m = jnp.maximum(m, sk)
            p = jnp.exp(s - m)
            l = jnp.sum(p, axis=-1, keepdims=True)
            if has_sink:
                l = l + jnp.exp(sk - m)
            o = _dot(p.astype(BF16), vw) / l
            o_ref[sb * SUB:(sb + 1) * SUB, g * HEAD_DIM:(g + 1) * HEAD_DIM] = o.astype(o_ref.dtype)


def band_prompt(proj, *, q_col, k_col, v_col, n_kv, group, halo, n_band, bias=None, sink=None):
    s = proj.shape[0]
    qb = BAND_QB
    assert s % qb == 0 and qb % halo == 0 and halo % CHUNK == 0
    qw = group * HEAD_DIM
    hpb = qb // halo
    qc, kc, vc = q_col // qw, k_col // HEAD_DIM, v_col // HEAD_DIM
    main = lambda c0: pl.BlockSpec((qb, HEAD_DIM), lambda h, i: (i, c0 + h))
    halo_spec = lambda c0: pl.BlockSpec((halo, HEAD_DIM), lambda h, i: (jnp.maximum(i * hpb - 1, 0), c0 + h))
    in_specs = [pl.BlockSpec((qb, qw), lambda h, i: (i, qc + h)), main(kc), halo_spec(kc), main(vc), halo_spec(vc)]
    args = [proj, proj, proj, proj, proj]
    if bias is not None:
        in_specs.append(pl.BlockSpec((1, SUB, halo + SUB), lambda h, i: (h, 0, 0)))
        args.append(bias)
    if sink is not None:
        in_specs.append(pl.BlockSpec(memory_space=pltpu.SMEM))
        args.append(sink)
    return pl.pallas_call(
        functools.partial(_band_prompt_kernel, group=group, halo=halo, n_band=n_band,
                          has_bias=bias is not None, has_sink=sink is not None),
        grid=(n_kv, s // qb),
        in_specs=in_specs,
        out_specs=pl.BlockSpec((qb, qw), lambda h, i: (i, h)),
        out_shape=jax.ShapeDtypeStruct((s, n_kv * qw), BF16),
        compiler_params=_params(("parallel", "arbitrary")),
        name="band_prompt",
    )(*args)


def _band_sample_kernel(*refs, group, n_kv, has_bias, has_sink):
    q_ref, kn_ref, vn_ref, kc_ref, vc_ref = refs[:5]
    rest = list(refs[5:])
    bias_ref = rest.pop(0) if has_bias else None
    sink_ref = rest.pop(0) if has_sink else None
    o_ref, ko_ref, vo_ref = rest[-3:]
    t = q_ref.shape[0]
    lc = kc_ref.shape[0]
    ko_ref[:lc - t] = kc_ref[t:]
    vo_ref[:lc - t] = vc_ref[t:]
    if has_sink:
        row = lax.broadcasted_iota(jnp.int32, (group * t, 1), 0)
    for h in range(n_kv):
        cols = slice(h * HEAD_DIM, (h + 1) * HEAD_DIM)
        kn = kn_ref[:, cols]
        vn = vn_ref[:, cols]
        ko_ref[lc - t:, h, :] = kn
        vo_ref[lc - t:, h, :] = vn
        kc = kc_ref[:, h, :].astype(BF16)
        vc = vc_ref[:, h, :].astype(BF16)
        heads = [h * group + g for g in range(group)]
        q = jnp.concatenate([q_ref[:, j * HEAD_DIM:(j + 1) * HEAD_DIM] for j in heads], axis=0).astype(BF16)
        sc = _dot_t(q, kc)
        sn = _dot_t(q, kn.astype(BF16))
        if has_bias:
            sc = sc + bias_ref[h, :t, :lc]
            sn = sn + bias_ref[h, :t, lc:lc + t]
        m = jnp.maximum(jnp.max(sc, axis=-1, keepdims=True), jnp.max(sn, axis=-1, keepdims=True))
        if has_sink:
            sk = jnp.zeros((group * t, 1), F32)
            for g, j in enumerate(heads):
                sk = jnp.where((row >= g * t) & (row < (g + 1) * t), sink_ref[j], sk)
            m = jnp.maximum(m, sk)
        pc = jnp.exp(sc - m)
        pn = jnp.exp(sn - m)
        l = jnp.sum(pc, axis=-1, keepdims=True) + jnp.sum(pn, axis=-1, keepdims=True)
        if has_sink:
            l = l + jnp.exp(sk - m)
        o = (_dot(pc.astype(BF16), vc) + _dot(pn.astype(BF16), vn.astype(BF16))) / l
        for g, j in enumerate(heads):
            o_ref[:, j * HEAD_DIM:(j + 1) * HEAD_DIM] = o[g * t:(g + 1) * t].astype(o_ref.dtype)


def band_sample(q, k_new, v_new, cache_k, cache_v, stack_k, stack_v, layer, *, t, group, bias=None, sink=None):
    _, nb, lc, n_kv, _ = cache_k.shape
    rows = lambda width: pl.BlockSpec((t, width), lambda b: (b, 0))
    cache_spec = pl.BlockSpec((None, None, lc, n_kv, HEAD_DIM), lambda b: (layer, b, 0, 0, 0))
    in_specs = [rows(q.shape[1]), rows(k_new.shape[1]), rows(v_new.shape[1]), cache_spec, cache_spec]
    args = [q, k_new, v_new, cache_k, cache_v]
    if bias is not None:
        assert lc == B_REACH and t <= CHUNK
        in_specs.append(pl.BlockSpec(bias.shape, lambda b: (0, 0, 0)))
        args.append(bias)
    if sink is not None:
        in_specs.append(pl.BlockSpec(memory_space=pltpu.SMEM))
        args.append(sink)
    aliases = {}
    if stack_k is not None:
        aliases = {len(args): 1, len(args) + 1: 2}
        in_specs += [pl.BlockSpec(memory_space=pl.ANY)] * 2
        args += [stack_k, stack_v]
    return pl.pallas_call(
        functools.partial(_band_sample_kernel, group=group, n_kv=n_kv, has_bias=bias is not None,
                          has_sink=sink is not None),
        grid=(nb,),
        in_specs=in_specs,
        out_specs=[rows(q.shape[1]), cache_spec, cache_spec],
        out_shape=[jax.ShapeDtypeStruct(q.shape, BF16),
                   jax.ShapeDtypeStruct(cache_k.shape, F32), jax.ShapeDtypeStruct(cache_v.shape, F32)],
        input_output_aliases=aliases,
        compiler_params=_params(("parallel",)),
        name="band_sample",
    )(*args)


def _tri2():
    j = lax.broadcasted_iota(jnp.int32, (SB_BLOCK, 2 * SB_BLOCK), 0)
    s = lax.broadcasted_iota(jnp.int32, (SB_BLOCK, 2 * SB_BLOCK), 1)
    return jnp.where((s >= SB_BLOCK) | (j > s), 1.0, 0.0).astype(BF16)


def _softplus(z):
    neg_abs = lax.bitcast_convert_type(lax.bitcast_convert_type(z, jnp.int32) | jnp.int32(-2 ** 31), F32)
    return jnp.maximum(z, 0.0) + jnp.log(1.0 + jnp.exp(neg_abs))


def _sb_tile(z, tri2, carry):
    nls = _softplus(z)
    la = _dot(nls.astype(BF16), tri2)
    logw = z - nls - la[:, :SB_BLOCK]
    if carry is None:
        return logw, la[:, SB_BLOCK:]
    return logw - carry, carry + la[:, SB_BLOCK:]


def _causal_tile_mask(rows):
    r = lax.broadcasted_iota(jnp.int32, (rows, SB_BLOCK), 0)
    c = lax.broadcasted_iota(jnp.int32, (rows, SB_BLOCK), 1)
    return c < r


def _sb_block(ref, j):
    return ref[pl.ds(pl.multiple_of(j * SB_BLOCK, SB_BLOCK), SB_BLOCK), :].astype(BF16)


def _sb_older_blocks(q, k_ref, v_ref, tri2, first, acc, carry):
    def cond(state):
        j, _, carry = state
        return (j >= 0) & (jnp.min(carry) < SB_SKIP_LOG)

    def body(state):
        j, acc, carry = state
        logw, carry = _sb_tile(_dot_t(q, _sb_block(k_ref, j)), tri2, carry)
        acc = acc + _dot(jnp.exp(logw).astype(BF16), _sb_block(v_ref, j))
        return j - 1, acc, carry

    _, acc, _ = lax.while_loop(cond, body, (first, acc, carry))
    return acc


def _sb_prompt_kernel(q_ref, k_ref, v_ref, o_ref):
    tq = q_ref.shape[0]
    nsub = tq // SB_BLOCK
    nwin = nsub + SB_LOOKBACK
    pid = pl.program_id(1)
    base = pid * nsub
    is_first = pid == 0
    start = pl.multiple_of(jnp.maximum(base - SB_LOOKBACK, 0) * SB_BLOCK, SB_BLOCK)
    q = q_ref[...]
    kw = k_ref[pl.ds(start, nwin * SB_BLOCK), :]
    vw = v_ref[pl.ds(start, nwin * SB_BLOCK), :]
    kw = jnp.where(is_first, jnp.roll(kw, SB_LOOKBACK * SB_BLOCK, axis=0), kw)
    vw = jnp.where(is_first, jnp.roll(vw, SB_LOOKBACK * SB_BLOCK, axis=0), vw)
    tri2 = _tri2()
    z = _dot_t(q, kw)
    diag = _causal_tile_mask(SB_BLOCK)
    accs, carries = [], []
    for a in range(nsub):
        rows = slice(a * SB_BLOCK, (a + 1) * SB_BLOCK)
        carry = None
        pieces = []
        for b in range(a + SB_LOOKBACK, -1, -1):
            zt = z[rows, b * SB_BLOCK:(b + 1) * SB_BLOCK]
            if b == a + SB_LOOKBACK:
                zt = jnp.where(diag, zt, NEG_INF)
            if b < SB_LOOKBACK:
                zt = jnp.where(is_first, NEG_INF, zt)
            logw, carry = _sb_tile(zt, tri2, carry)
            pieces.insert(0, logw)
        w = jnp.exp(jnp.concatenate(pieces, axis=1)).astype(BF16)
        accs.append(_dot(w, vw[:(a + SB_LOOKBACK + 1) * SB_BLOCK]))
        carries.append(carry)
    acc = jnp.concatenate(accs, axis=0)
    carry = jnp.concatenate(carries, axis=0)
    acc = _sb_older_blocks(q, k_ref, v_ref, tri2, base - SB_LOOKBACK - 1, acc, carry)
    o_ref[...] = acc.astype(o_ref.dtype)


def sb_prompt(q, k, v):
    s = q.shape[0]
    tq = min(SB_QUERY_ROWS, s)
    assert s % tq == 0 and tq % SB_BLOCK == 0 and s >= tq + SB_LOOKBACK * SB_BLOCK
    head_all = pl.BlockSpec((s, HEAD_DIM), lambda h, i: (0, h))
    return pl.pallas_call(
        _sb_prompt_kernel,
        grid=(HC, s // tq),
        in_specs=[pl.BlockSpec((tq, HEAD_DIM), lambda h, i: (i, h)), head_all, head_all],
        out_specs=pl.BlockSpec((tq, HEAD_DIM), lambda h, i: (i, h)),
        out_shape=jax.ShapeDtypeStruct(q.shape, BF16),
        compiler_params=_params(("parallel", "arbitrary")),
        name="sb_prompt",
    )(q, k, v)


def _sb_sample_kernel(q_ref, kn_ref, vn_ref, kc_ref, vc_ref, o_ref, acc_ref, carry_ref):
    s = pl.program_id(1)
    t = q_ref.shape[0]
    n_heads = acc_ref.shape[0]
    tri2 = _tri2()

    def head_q(h):
        return q_ref[:, h * HEAD_DIM:(h + 1) * HEAD_DIM].astype(BF16)

    @pl.when(s == 0)
    def _():
        zeros = jnp.zeros((SB_BLOCK - t, HEAD_DIM), BF16)
        mask = _causal_tile_mask(t)
        for h in range(n_heads):
            cols = slice(h * HEAD_DIM, (h + 1) * HEAD_DIM)
            kb = jnp.concatenate([kn_ref[:, cols].astype(BF16), zeros], axis=0)
            vb = jnp.concatenate([vn_ref[:, cols].astype(BF16), zeros], axis=0)
            logw, carry = _sb_tile(jnp.where(mask, _dot_t(head_q(h), kb), NEG_INF), tri2, None)
            acc_ref[h] = _dot(jnp.exp(logw).astype(BF16), vb)
            carry_ref[h] = carry

    @pl.when(jnp.min(carry_ref[...]) < SB_SKIP_LOG)
    def _():
        for h in range(n_heads):
            kb = kc_ref[:, h, :].astype(BF16)
            vb = vc_ref[:, h, :].astype(BF16)
            logw, carry = _sb_tile(_dot_t(head_q(h), kb), tri2, carry_ref[h])
            acc_ref[h] += _dot(jnp.exp(logw).astype(BF16), vb)
            carry_ref[h] = carry

    @pl.when(s == pl.num_programs(1) - 1)
    def _():
        for h in range(n_heads):
            o_ref[:, h * HEAD_DIM:(h + 1) * HEAD_DIM] = acc_ref[h].astype(o_ref.dtype)


def sb_sample(q, k_new, v_new, cache_k, cache_v, layer, *, t):
    _, nb, past, n_heads, _ = cache_k.shape
    assert past % SB_BLOCK == 0 and t <= SB_BLOCK
    nblk = past // SB_BLOCK
    rows = pl.BlockSpec((t, n_heads * HEAD_DIM), lambda b, s: (b, 0))
    cache_spec = pl.BlockSpec((None, None, SB_BLOCK, n_heads, HEAD_DIM), lambda b, s: (layer, b, nblk - 1 - s, 0, 0))
    return pl.pallas_call(
        _sb_sample_kernel,
        grid=(nb, nblk),
        in_specs=[rows, rows, rows, cache_spec, cache_spec],
        out_specs=rows,
        out_shape=jax.ShapeDtypeStruct(q.shape, BF16),
        scratch_shapes=[pltpu.VMEM((n_heads, t, HEAD_DIM), F32), pltpu.VMEM((n_heads, t, HEAD_DIM), F32)],
        compiler_params=_params(("parallel", "arbitrary")),
        name="sb_sample",
    )(q, k_new, v_new, cache_k, cache_v)


def _rope_table_kernel(inv_ref, cos_ref, sin_ref, *, pos0):
    tm = cos_ref.shape[0]
    pos = lax.broadcasted_iota(jnp.int32, (tm, HEAD_DIM), 0) + (pos0 + pl.program_id(0) * tm)
    lane = lax.broadcasted_iota(jnp.int32, (tm, HEAD_DIM), 1)
    ang = pos.astype(F32) * inv_ref[...]
    sin = jnp.sin(ang)
    cos_ref[...] = jnp.cos(ang)
    sin_ref[...] = jnp.where(lane < HEAD_DIM // 2, -sin, sin)


def _rope_tables(n, pos0):
    half = HEAD_DIM // 2
    inv = ROPE_THETA ** (-jnp.arange(half, dtype=F32) / half)
    inv = jnp.concatenate([inv, inv]).reshape(1, HEAD_DIM)
    tm = min(1024, n)
    assert n % tm == 0
    return pl.pallas_call(
        functools.partial(_rope_table_kernel, pos0=pos0),
        grid=(n // tm,),
        in_specs=[pl.BlockSpec((1, HEAD_DIM), lambda i: (0, 0))],
        out_specs=[pl.BlockSpec((tm, HEAD_DIM), lambda i: (i, 0))] * 2,
        out_shape=[jax.ShapeDtypeStruct((n, HEAD_DIM), F32)] * 2,
        compiler_params=_params(("parallel",)),
        name="rope_tables",
    )(inv)


def kernel(x_prompt, x_sample, cache_a_k, cache_a_v, cache_b_k, cache_b_v, cache_c_k, cache_c_v, norm_mix, w_in_ab, sink_a, rel_bias_b, w_out_ab, w_in_c, w_out_c, norm_ffn, w_gate, w_up, w_down, norm_final):
    bp, seq, d = x_prompt.shape
    nb, t, _ = x_sample.shape
    depth = norm_mix.shape[0]
    past = cache_c_k.shape[2]
    assert bp == 1 and seq >= B_REACH
    keep_a, keep_b = min(A_REACH, seq), min(B_REACH, seq)
    wc = HC * HEAD_DIM

    yp = x_prompt.reshape(seq, d)
    ys = x_sample.reshape(nb * t, d)
    cos_p, sin_p = _rope_tables(seq, 0)
    cos_s, sin_s = _rope_tables(t, past)
    cos_s, sin_s = jnp.tile(cos_s, (nb, 1)), jnp.tile(sin_s, (nb, 1))

    outs = {name: [] for name in ("pa_k", "pa_v", "pb_k", "pb_v", "sc_k", "sc_v")}
    n_c = cache_c_k.shape[0]
    sa_k = sa_v = sb_k = sb_v = pc_k = pc_v = None
    hp, qp = norm_inputs(yp)
    hs, qs_ = norm_inputs(ys)

    def weight_jobs(layer):
        i = layer // 2
        if layer % 2 == 0:
            mixer = [CastJob(w_in_ab, i, norm_mix[layer], AB_TILE), CastJob(w_out_ab, i, None, OUT_TILE)]
        else:
            mixer = [CastJob(w_in_c, i, norm_mix[layer], None), CastJob(w_out_c, i, None, None)]
        return mixer + [CastJob(w_gate, layer, norm_ffn[layer], FF_TILE), CastJob(w_up, layer, norm_ffn[layer], FF_TILE)]

    w_in, w_out, wg, wu = [cast_bf16(*job) for job in weight_jobs(0)]
    for layer in range(depth):
        i = layer // 2
        if layer % 2 == 0:
            bias = bias_prep(rel_bias_b[i])
            proj = matmul_ab(hp, qp, w_in, cos_p, sin_p, BF16)
            oa = band_prompt(proj, q_col=0, k_col=AB_SPLITS[0], v_col=AB_SPLITS[1], n_kv=KV_A, group=G_A,
                             halo=A_REACH, n_band=A_BAND_CHUNKS, sink=sink_a[i])
            ob = band_prompt(proj, q_col=AB_SPLITS[2], k_col=AB_SPLITS[3], v_col=AB_SPLITS[4], n_kv=HB, group=1,
                             halo=B_REACH, n_band=B_BAND_CHUNKS, bias=bias)
            tail = matmul_ab(hp[seq - keep_b:], qp[seq - keep_b:], w_in, cos_p[seq - keep_b:], sin_p[seq - keep_b:], F32)
            yp, hp, qp = matmul2_res(oa, ob, w_out, yp)
            outs["pa_k"].append(tail[keep_b - keep_a:, AB_SPLITS[0]:AB_SPLITS[1]].reshape(1, keep_a, KV_A, HEAD_DIM))
            outs["pa_v"].append(tail[keep_b - keep_a:, AB_SPLITS[1]:AB_SPLITS[2]].reshape(1, keep_a, KV_A, HEAD_DIM))
            outs["pb_k"].append(tail[:, AB_SPLITS[3]:AB_SPLITS[4]].reshape(1, keep_b, HB, HEAD_DIM))
            outs["pb_v"].append(tail[:, AB_SPLITS[4]:].reshape(1, keep_b, HB, HEAD_DIM))
            projs = matmul_ab(hs, qs_, w_in, cos_s, sin_s, F32)
            qa, ka, va, qb, kb, vb = jnp.split(projs, AB_SPLITS, axis=1)
            oa, sa_k, sa_v = band_sample(qa, ka, va, cache_a_k, cache_a_v, sa_k, sa_v, i, t=t, group=G_A,
                                         sink=sink_a[i])
            ob, sb_k, sb_v = band_sample(qb, kb, vb, cache_b_k, cache_b_v, sb_k, sb_v, i, t=t, group=1, bias=bias)
            ys, hs, qs_ = matmul2_res(oa, ob, w_out, ys)
        else:
            q = matmul(hp, qp, w_in, BF16, scale=QK_SCALE, col0=0, n=wc)
            pc_k, k16 = matmul_heads(hp, qp, w_in, pc_k, i, n_c, col0=wc, n=wc)
            pc_v, v16 = matmul_heads(hp, qp, w_in, pc_v, i, n_c, col0=2 * wc, n=wc)
            att = sb_prompt(q, k16, v16)
            yp, hp, qp = matmul_res(att, w_out, yp, tm=1024, tn=512)
            q = matmul(hs, qs_, w_in, F32, scale=QK_SCALE, col0=0, n=wc)
            ks = matmul(hs, qs_, w_in, F32, col0=wc, n=wc)
            vs = matmul(hs, qs_, w_in, F32, col0=2 * wc, n=wc)
            att = sb_sample(q, ks, vs, cache_c_k, cache_c_v, i, t=t)
            ys, hs, qs_ = matmul_res(att, w_out, ys, tm=1024, tn=512)
            outs["sc_k"].append(ks.reshape(nb, t, HC, HEAD_DIM))
            outs["sc_v"].append(vs.reshape(nb, t, HC, HEAD_DIM))
        jobs = [CastJob(w_down, layer, None, OUT_TILE)] + (weight_jobs(layer + 1) if layer + 1 < depth else [])
        act_p, casts = gateup(hp, qp, wg, wu, jobs)
        act_s, _ = gateup(hs, qs_, wg, wu)
        wd = casts[0]
        yp, hp, qp = matmul_res(act_p, wd, yp)
        ys, hs, qs_ = matmul_res(act_s, wd, ys)
        if layer + 1 < depth:
            w_in, w_out, wg, wu = casts[1:]

    y_prompt = rmsnorm(yp, norm_final, F32).reshape(1, seq, d)
    y_sample = rmsnorm(ys, norm_final, F32).reshape(nb, t, d)
    st = {name: jnp.stack(v) for name, v in outs.items()}
    pc_k = pc_k.reshape(n_c, 1, seq, HC, HEAD_DIM)
    pc_v = pc_v.reshape(n_c, 1, seq, HC, HEAD_DIM)
    return (y_prompt, y_sample, st["pa_k"], st["pa_v"], st["pb_k"], st["pb_v"], pc_k, pc_v,
            sa_k, sa_v, sb_k, sb_v, st["sc_k"], st["sc_v"])
```

```python
import functools
from typing import NamedTuple, Optional

import jax
import jax.numpy as jnp
from jax import lax
from jax.experimental import pallas as pl
from jax.experimental.pallas import tpu as pltpu

HEAD_DIM = 128
CHUNK = 64
HA = 16
KV_A = 4
G_A = HA // KV_A
HB = 16
HC = 16
A_BAND_CHUNKS = 3
B_BAND_CHUNKS = 9
A_REACH = (A_BAND_CHUNKS - 1) * CHUNK
B_REACH = (B_BAND_CHUNKS - 1) * CHUNK
REL_CLIP = 256
N_REL = 2 * REL_CLIP + 1
SB_BLOCK = 128
ROPE_THETA = 10000.0
RMS_EPS = 1e-6
NEG_INF = -1e30
QA_W = HA * HEAD_DIM
KA_W = KV_A * HEAD_DIM
QB_W = HB * HEAD_DIM
AB_SPLITS = (QA_W, QA_W + KA_W, QA_W + 2 * KA_W, QA_W + 2 * KA_W + QB_W, QA_W + 2 * KA_W + 2 * QB_W)
QK_SCALE = HEAD_DIM ** -0.5

SB_QUERY_ROWS = 512
SB_LOOKBACK = 2
SB_SKIP_LOG = 106.0
BF16_ROWS = 16
AB_TILE = 512
OUT_TILE = 512
FF_TILE = 256
SUB = 128
BAND_QB = 1024
VMEM_LIMIT_MB = 56

F32 = jnp.float32
BF16 = jnp.bfloat16


def _params(semantics):
    return pltpu.CompilerParams(dimension_semantics=semantics, vmem_limit_bytes=VMEM_LIMIT_MB << 20)


def _dot(a, b):
    return jnp.dot(a, b, preferred_element_type=F32)


def _dot_t(a, b):
    return lax.dot_general(a, b, (((1,), (1,)), ((), ())), preferred_element_type=F32)


def _rmsnorm_kernel(x_ref, g_ref, o_ref):
    x = x_ref[...]
    ms = jnp.mean(x * x, axis=-1, keepdims=True)
    o_ref[...] = (x * lax.rsqrt(ms + RMS_EPS) * g_ref[...]).astype(o_ref.dtype)


def rmsnorm(x, g, out_dtype):
    m, d = x.shape
    tm = min(256, m)
    return pl.pallas_call(
        _rmsnorm_kernel,
        grid=(m // tm,),
        in_specs=[pl.BlockSpec((tm, d), lambda i: (i, 0)), pl.BlockSpec((1, d), lambda i: (0, 0))],
        out_specs=pl.BlockSpec((tm, d), lambda i: (i, 0)),
        out_shape=jax.ShapeDtypeStruct((m, d), out_dtype),
        compiler_params=_params(("parallel",)),
        name="rmsnorm",
    )(x, g.reshape(1, d))


def _lane_partial_ssq(y):
    sq = y * y
    part = sq[:, :HEAD_DIM]
    for c in range(1, y.shape[1] // HEAD_DIM):
        part = part + sq[:, c * HEAD_DIM:(c + 1) * HEAD_DIM]
    return part


def _row_rstd(ssq_ref, d):
    return lax.rsqrt(jnp.sum(ssq_ref[...], axis=-1, keepdims=True) / d + RMS_EPS)


def _norm_inputs_kernel(x_ref, o16_ref, ssq_ref):
    x = x_ref[...]
    o16_ref[...] = x.astype(o16_ref.dtype)
    ssq_ref[...] = _lane_partial_ssq(x)


def norm_inputs(x):
    m, d = x.shape
    tm = min(256, m)
    return pl.pallas_call(
        _norm_inputs_kernel,
        grid=(m // tm,),
        in_specs=[pl.BlockSpec((tm, d), lambda i: (i, 0))],
        out_specs=[pl.BlockSpec((tm, d), lambda i: (i, 0)), pl.BlockSpec((tm, HEAD_DIM), lambda i: (i, 0))],
        out_shape=[jax.ShapeDtypeStruct((m, d), BF16), jax.ShapeDtypeStruct((m, HEAD_DIM), F32)],
        compiler_params=_params(("parallel",)),
        name="norm_inputs",
    )(x)


def _cast_block(x_ref, gain_ref, o_ref):
    x = x_ref[...]
    if gain_ref is not None:
        x = x * gain_ref[...]
    x = x.astype(o_ref.dtype)
    if len(o_ref.shape) == 2:
        o_ref[...] = x
    else:
        ct = o_ref.shape[2]
        for c in range(o_ref.shape[0]):
            o_ref[c] = x[:, c * ct:(c + 1) * ct]


def _cast_kernel(x_ref, *rest):
    _cast_block(x_ref, rest[0] if len(rest) == 2 else None, rest[-1])


class CastJob(NamedTuple):
    stack: jax.Array
    layer: int
    gain: Optional[jax.Array]
    col_tile: Optional[int]


def _cast_job_specs(job, n_steps, step_of):
    _, k, n = job.stack.shape
    tr = next(r for r in range(BF16_ROWS, k + 1, BF16_ROWS) if k % r == 0 and k // r <= n_steps)
    last = k // tr - 1
    blk = lambda i, j: jnp.minimum(step_of(i, j), last)
    layer = job.layer
    in_specs = [pl.BlockSpec((None, tr, n), lambda i, j: (layer, blk(i, j), 0))]
    args = [job.stack]
    if job.gain is not None:
        in_specs.append(pl.BlockSpec((tr, 1), lambda i, j: (blk(i, j), 0)))
        args.append(job.gain.reshape(k, 1))
    if job.col_tile is None:
        out_spec = pl.BlockSpec((tr, n), lambda i, j: (blk(i, j), 0))
        out_shape = jax.ShapeDtypeStruct((k, n), BF16)
    else:
        ct = job.col_tile
        out_spec = pl.BlockSpec((n // ct, tr, ct), lambda i, j: (0, blk(i, j), 0))
        out_shape = jax.ShapeDtypeStruct((n // ct, k, ct), BF16)
    return in_specs, args, out_spec, out_shape, last + 1


def cast_bf16(w_stack, layer, gain=None, col_tile=None):
    _, k, n = w_stack.shape
    tr = min(256, k)
    assert k % tr == 0
    in_specs = [pl.BlockSpec((None, tr, n), lambda i: (layer, i, 0))]
    args = [w_stack]
    if gain is not None:
        in_specs.append(pl.BlockSpec((tr, 1), lambda i: (i, 0)))
        args.append(gain.reshape(k, 1))
    if col_tile is None:
        out_spec = pl.BlockSpec((tr, n), lambda i: (i, 0))
        out_shape = jax.ShapeDtypeStruct((k, n), BF16)
    else:
        assert n % col_tile == 0
        out_spec = pl.BlockSpec((n // col_tile, tr, col_tile), lambda i: (0, i, 0))
        out_shape = jax.ShapeDtypeStruct((n // col_tile, k, col_tile), BF16)
    return pl.pallas_call(
        _cast_kernel,
        grid=(k // tr,),
        in_specs=in_specs,
        out_specs=out_spec,
        out_shape=out_shape,
        compiler_params=_params(("parallel",)),
        name="cast_bf16",
    )(*args)


def _w_spec(w, k, tn, c0=0):
    if w.ndim == 2:
        return pl.BlockSpec((k, tn), lambda i, j: (0, c0 + j))
    assert w.shape[1:] == (k, tn), (w.shape, k, tn)
    return pl.BlockSpec((None, k, tn), lambda i, j: (c0 + j, 0, 0))


def _w_cols(w):
    return w.shape[1] if w.ndim == 2 else w.shape[0] * w.shape[2]


def _mm_kernel(a_ref, ssq_ref, w_ref, o_ref, *, scale):
    acc = _dot(a_ref[...], w_ref[...]) * (_row_rstd(ssq_ref, a_ref.shape[1]) * scale)
    o_ref[...] = acc.astype(o_ref.dtype)


def _mm_heads_kernel(a_ref, ssq_ref, w_ref, *rest):
    o32_ref, o16_ref = rest[-2:]
    acc = _dot(a_ref[...], w_ref[...]) * _row_rstd(ssq_ref, a_ref.shape[1])
    for h in range(o32_ref.shape[1]):
        o32_ref[:, h, :] = acc[:, h * HEAD_DIM:(h + 1) * HEAD_DIM]
    o16_ref[...] = acc.astype(o16_ref.dtype)


def _emit_residual(y, o_ref, o16_ref, ssq_ref):
    o_ref[...] = y
    o16_ref[...] = y.astype(o16_ref.dtype)
    part = _lane_partial_ssq(y)
    first = pl.program_id(1) == 0

    @pl.when(first)
    def _():
        ssq_ref[...] = part

    @pl.when(jnp.logical_not(first))
    def _():
        ssq_ref[...] += part


def _mm_res_kernel(a_ref, w_ref, r_ref, o_ref, o16_ref, ssq_ref):
    _emit_residual(r_ref[...] + _dot(a_ref[...], w_ref[...]), o_ref, o16_ref, ssq_ref)


def _mm2_res_kernel(a1_ref, a2_ref, w_ref, r_ref, o_ref, o16_ref, ssq_ref):
    k1 = a1_ref.shape[1]
    acc = _dot(a1_ref[...], w_ref[:k1, :]) + _dot(a2_ref[...], w_ref[k1:, :])
    _emit_residual(r_ref[...] + acc, o_ref, o16_ref, ssq_ref)


def _gateup_kernel(a_ref, ssq_ref, wg_ref, wu_ref, *rest, job_has_gain, job_blocks):
    n_jobs = len(job_has_gain)
    o_ref = rest[len(rest) - n_jobs - 1]
    a = a_ref[...]
    rstd = _row_rstd(ssq_ref, a_ref.shape[1])
    g = _dot(a, wg_ref[...]) * rstd
    u = _dot(a, wu_ref[...]) * rstd
    o_ref[...] = (g / (1.0 + jnp.exp(-g)) * u).astype(o_ref.dtype)
    step = pl.program_id(0) * pl.num_programs(1) + pl.program_id(1)
    pos = 0
    for job, (has_gain, n_blocks) in enumerate(zip(job_has_gain, job_blocks)):
        x_ref = rest[pos]
        gain_ref = rest[pos + 1] if has_gain else None
        pos += 2 if has_gain else 1
        pl.when(step < n_blocks)(functools.partial(_cast_block, x_ref, gain_ref, rest[len(rest) - n_jobs + job]))


def _mm_ab_kernel(a_ref, ssq_ref, w_ref, cos_ref, sin_ref, o_ref, *, tn):
    col0 = pl.program_id(1) * tn
    acc = _dot(a_ref[...], w_ref[...]) * _row_rstd(ssq_ref, a_ref.shape[1])
    is_rope = col0 < AB_SPLITS[1]
    is_q = (col0 < AB_SPLITS[0]) | ((col0 >= AB_SPLITS[2]) & (col0 < AB_SPLITS[3]))
    s = jnp.where(is_q, QK_SCALE, 1.0).astype(F32)

    @pl.when(is_rope)
    def _():
        cos = cos_ref[...]
        sin = sin_ref[...]
        for g in range(tn // HEAD_DIM):
            blk = acc[:, g * HEAD_DIM:(g + 1) * HEAD_DIM]
            rot = blk * cos + pltpu.roll(blk, HEAD_DIM // 2, axis=1) * sin
            o_ref[:, g * HEAD_DIM:(g + 1) * HEAD_DIM] = (rot * s).astype(o_ref.dtype)

    @pl.when(jnp.logical_not(is_rope))
    def _():
        o_ref[...] = (acc * s).astype(o_ref.dtype)


def _tiles(m, n, tm, tn):
    tm = min(tm, m)
    tn = min(tn, n)
    assert m % tm == 0 and n % tn == 0, (m, n, tm, tn)
    return tm, tn


def _ssq_spec(tm):
    return pl.BlockSpec((tm, HEAD_DIM), lambda i, j: (i, 0))


def _residual_out_specs(tm, tn):
    return [pl.BlockSpec((tm, tn), lambda i, j: (i, j)), pl.BlockSpec((tm, tn), lambda i, j: (i, j)), _ssq_spec(tm)]


def _residual_out_shapes(m, n):
    return [jax.ShapeDtypeStruct((m, n), F32), jax.ShapeDtypeStruct((m, n), BF16),
            jax.ShapeDtypeStruct((m, HEAD_DIM), F32)]


def matmul(a, ssq, w, out_dtype, *, scale=1.0, col0=0, n=None, tm=1024, tn=512):
    m, k = a.shape
    n = w.shape[1] - col0 if n is None else n
    tm, tn = _tiles(m, n, tm, tn)
    assert col0 % tn == 0
    c0 = col0 // tn
    return pl.pallas_call(
        functools.partial(_mm_kernel, scale=scale),
        grid=(m // tm, n // tn),
        in_specs=[pl.BlockSpec((tm, k), lambda i, j: (i, 0)), _ssq_spec(tm),
                  pl.BlockSpec((k, tn), lambda i, j: (0, c0 + j))],
        out_specs=pl.BlockSpec((tm, tn), lambda i, j: (i, j)),
        out_shape=jax.ShapeDtypeStruct((m, n), out_dtype),
        compiler_params=_params(("parallel", "arbitrary")),
        name="matmul",
    )(a, ssq, w)


def matmul_heads(a, ssq, w, stack, layer, n_layers, *, col0, n, tm=1024, tn=1024):
    m, k = a.shape
    tm, tn = _tiles(m, n, tm, tn)
    assert col0 % tn == 0 and tn % (8 * HEAD_DIM) == 0
    c0 = col0 // tn
    hpt = tn // HEAD_DIM
    in_specs = [pl.BlockSpec((tm, k), lambda i, j: (i, 0)), _ssq_spec(tm),
                pl.BlockSpec((k, tn), lambda i, j: (0, c0 + j))]
    args = [a, ssq, w]
    if stack is not None:
        in_specs.append(pl.BlockSpec(memory_space=pl.ANY))
        args.append(stack)
    return pl.pallas_call(
        _mm_heads_kernel,
        grid=(m // tm, n // tn),
        in_specs=in_specs,
        out_specs=[pl.BlockSpec((None, tm, hpt, HEAD_DIM), lambda i, j: (layer, i, j, 0)),
                   pl.BlockSpec((tm, tn), lambda i, j: (i, j))],
        out_shape=[jax.ShapeDtypeStruct((n_layers, m, n // HEAD_DIM, HEAD_DIM), F32), jax.ShapeDtypeStruct((m, n), BF16)],
        input_output_aliases={} if stack is None else {3: 0},
        compiler_params=_params(("parallel", "arbitrary")),
        name="matmul_heads",
    )(*args)


def matmul_res(a, w, res, *, tm=512, tn=OUT_TILE):
    m, k = a.shape
    n = _w_cols(w)
    tm, tn = _tiles(m, n, tm, tn)
    return pl.pallas_call(
        _mm_res_kernel,
        grid=(m // tm, n // tn),
        in_specs=[pl.BlockSpec((tm, k), lambda i, j: (i, 0)), _w_spec(w, k, tn),
                  pl.BlockSpec((tm, tn), lambda i, j: (i, j))],
        out_specs=_residual_out_specs(tm, tn),
        out_shape=_residual_out_shapes(m, n),
        compiler_params=_params(("parallel", "arbitrary")),
        name="matmul_res",
    )(a, w, res)


def matmul2_res(a1, a2, w, res, *, tm=1024, tn=OUT_TILE):
    m, k1 = a1.shape
    k2 = a2.shape[1]
    n = _w_cols(w)
    tm, tn = _tiles(m, n, tm, tn)
    return pl.pallas_call(
        _mm2_res_kernel,
        grid=(m // tm, n // tn),
        in_specs=[pl.BlockSpec((tm, k1), lambda i, j: (i, 0)), pl.BlockSpec((tm, k2), lambda i, j: (i, 0)),
                  _w_spec(w, k1 + k2, tn), pl.BlockSpec((tm, tn), lambda i, j: (i, j))],
        out_specs=_residual_out_specs(tm, tn),
        out_shape=_residual_out_shapes(m, n),
        compiler_params=_params(("parallel", "arbitrary")),
        name="matmul2_res",
    )(a1, a2, w, res)


def gateup(a, ssq, wg, wu, cast_jobs=(), *, tm=1024, tn=FF_TILE):
    m, k = a.shape
    n = _w_cols(wg)
    tm, tn = _tiles(m, n, tm, tn)
    grid = (m // tm, n // tn)
    in_specs = [pl.BlockSpec((tm, k), lambda i, j: (i, 0)), _ssq_spec(tm), _w_spec(wg, k, tn), _w_spec(wu, k, tn)]
    args = [a, ssq, wg, wu]
    out_specs = [pl.BlockSpec((tm, tn), lambda i, j: (i, j))]
    out_shape = [jax.ShapeDtypeStruct((m, n), BF16)]
    job_blocks = []
    for job in cast_jobs:
        j_in, j_args, j_out, j_shape, n_blocks = _cast_job_specs(job, grid[0] * grid[1], lambda i, j: i * grid[1] + j)
        in_specs += j_in
        args += j_args
        out_specs.append(j_out)
        out_shape.append(j_shape)
        job_blocks.append(n_blocks)
    outs = pl.pallas_call(
        functools.partial(_gateup_kernel, job_has_gain=tuple(job.gain is not None for job in cast_jobs),
                          job_blocks=tuple(job_blocks)),
        grid=grid,
        in_specs=in_specs,
        out_specs=out_specs,
        out_shape=out_shape,
        compiler_params=_params(("parallel", "arbitrary")),
        name="gateup",
    )(*args)
    return outs[0], list(outs[1:])


def matmul_ab(a, ssq, w, cos, sin, out_dtype, *, tm=2048):
    m, k = a.shape
    n = _w_cols(w)
    tn = AB_TILE
    assert all(s % tn == 0 for s in AB_SPLITS)
    tm, tn = _tiles(m, n, tm, tn)
    return pl.pallas_call(
        functools.partial(_mm_ab_kernel, tn=tn),
        grid=(m // tm, n // tn),
        in_specs=[pl.BlockSpec((tm, k), lambda i, j: (i, 0)), _ssq_spec(tm), _w_spec(w, k, tn),
                  pl.BlockSpec((tm, HEAD_DIM), lambda i, j: (i, 0)), pl.BlockSpec((tm, HEAD_DIM), lambda i, j: (i, 0))],
        out_specs=pl.BlockSpec((tm, tn), lambda i, j: (i, j)),
        out_shape=jax.ShapeDtypeStruct((m, n), out_dtype),
        compiler_params=_params(("parallel", "arbitrary")),
        name="matmul_ab",
    )(a, ssq, w, cos, sin)


def _band_mask(rows, halo, n_band):
    r = lax.broadcasted_iota(jnp.int32, (rows, halo + SUB), 0)
    c = lax.broadcasted_iota(jnp.int32, (rows, halo + SUB), 1)
    rc = (r + halo) >> 6
    cc = c >> 6
    return (cc <= rc) & (cc > rc - n_band)


def _bias_prep_kernel(tab_ref, o_ref):
    h = pl.program_id(0)
    width = B_REACH + SUB
    ulen = width + SUB
    n = lax.broadcasted_iota(jnp.int32, (8, ulen), 1)
    idx = jnp.clip(B_REACH + SUB - 1 - n, -REL_CLIP, REL_CLIP) + REL_CLIP

    def body(m, u):
        return jnp.where(idx == m, tab_ref[h, m], u)

    u = lax.fori_loop(0, N_REL, body, jnp.zeros((8, ulen), F32))
    x = jnp.broadcast_to(u[0:1, :], (SUB, ulen))
    x = pltpu.roll(x, ulen - (SUB - 1), axis=1, stride=1, stride_axis=0)
    o_ref[0] = jnp.where(_band_mask(SUB, B_REACH, B_BAND_CHUNKS), x[:, :width], NEG_INF)


def bias_prep(table):
    return pl.pallas_call(
        _bias_prep_kernel,
        grid=(HB,),
        in_specs=[pl.BlockSpec(memory_space=pltpu.SMEM)],
        out_specs=pl.BlockSpec((1, SUB, B_REACH + SUB), lambda h: (h, 0, 0)),
        out_shape=jax.ShapeDtypeStruct((HB, SUB, B_REACH + SUB), F32),
        compiler_params=_params(("arbitrary",)),
        name="bias_prep",
    )(table)


def _band_prompt_kernel(*refs, group, halo, n_band, has_bias, has_sink):
    q_ref, km_ref, kh_ref, vm_ref, vh_ref = refs[:5]
    rest = list(refs[5:])
    bias_ref = rest.pop(0) if has_bias else None
    sink_ref = rest.pop(0) if has_sink else None
    o_ref = rest.pop(0)
    h = pl.program_id(0)
    i = pl.program_id(1)
    qb = q_ref.shape[0]
    width = halo + SUB
    k = jnp.concatenate([kh_ref[...], km_ref[...]], axis=0)
    v = jnp.concatenate([vh_ref[...], vm_ref[...]], axis=0)
    c = lax.broadcasted_iota(jnp.int32, (SUB, width), 1)
    if has_bias:
        bias = bias_ref[0]
    else:
        bias = jnp.where(_band_mask(SUB, halo, n_band), 0.0, NEG_INF).astype(F32)
    for sb in range(qb // SUB):
        kw = k[sb * SUB:sb * SUB + width]
        vw = v[sb * SUB:sb * SUB + width]
        first_valid = halo - i * qb - sb * SUB
        for g in range(group):
            q = q_ref[sb * SUB:(sb + 1) * SUB, g * HEAD_DIM:(g + 1) * HEAD_DIM]
            s = _dot_t(q, kw) + bias
            s = jnp.where(c >= first_valid, s, NEG_INF)
            m = jnp.max(s, axis=-1, keepdims=True)
            if has_sink:
                sk = sink_ref[h * group + g]
                m = jnp.maximum(m, sk)
            p = jnp.exp(s - m)
            l = jnp.sum(p, axis=-1, keepdims=True)
            if has_sink:
                l = l + jnp.exp(sk - m)
            o = _dot(p.astype(BF16), vw) / l
            o_ref[sb * SUB:(sb + 1) * SUB, g * HEAD_DIM:(g + 1) * HEAD_DIM] = o.astype(o_ref.dtype)


def band_prompt(proj, *, q_col, k_col, v_col, n_kv, group, halo, n_band, bias=None, sink=None):
    s = proj.shape[0]
    qb = BAND_QB
    assert s % qb == 0 and qb % halo == 0 and halo % CHUNK == 0
    qw = group * HEAD_DIM
    hpb = qb // halo
    qc, kc, vc = q_col // qw, k_col // HEAD_DIM, v_col // HEAD_DIM
    main = lambda c0: pl.BlockSpec((qb, HEAD_DIM), lambda h, i: (i, c0 + h))
    halo_spec = lambda c0: pl.BlockSpec((halo, HEAD_DIM), lambda h, i: (jnp.maximum(i * hpb - 1, 0), c0 + h))
    in_specs = [pl.BlockSpec((qb, qw), lambda h, i: (i, qc + h)), main(kc), halo_spec(kc), main(vc), halo_spec(vc)]
    args = [proj, proj, proj, proj, proj]
    if bias is not None:
        in_specs.append(pl.BlockSpec((1, SUB, halo + SUB), lambda h, i: (h, 0, 0)))
        args.append(bias)
    if sink is not None:
        in_specs.append(pl.BlockSpec(memory_space=pltpu.SMEM))
        args.append(sink)
    return pl.pallas_call(
        functools.partial(_band_prompt_kernel, group=group, halo=halo, n_band=n_band,
                          has_bias=bias is not None, has_sink=sink is not None),
        grid=(n_kv, s // qb),
        in_specs=in_specs,
        out_specs=pl.BlockSpec((qb, qw), lambda h, i: (i, h)),
        out_shape=jax.ShapeDtypeStruct((s, n_kv * qw), BF16),
        compiler_params=_params(("parallel", "arbitrary")),
        name="band_prompt",
    )(*args)


def _band_sample_kernel(*refs, group, n_kv, has_bias, has_sink):
    q_ref, kn_ref, vn_ref, kc_ref, vc_ref = refs[:5]
    rest = list(refs[5:])
    bias_ref = rest.pop(0) if has_bias else None
    sink_ref = rest.pop(0) if has_sink else None
    o_ref, ko_ref, vo_ref = rest[-3:]
    t = q_ref.shape[0]
    lc = kc_ref.shape[0]
    ko_ref[:lc - t] = kc_ref[t:]
    vo_ref[:lc - t] = vc_ref[t:]
    if has_sink:
        row = lax.broadcasted_iota(jnp.int32, (group * t, 1), 0)
    for h in range(n_kv):
        cols = slice(h * HEAD_DIM, (h + 1) * HEAD_DIM)
        kn = kn_ref[:, cols]
        vn = vn_ref[:, cols]
        ko_ref[lc - t:, h, :] = kn
        vo_ref[lc - t:, h, :] = vn
        kc = kc_ref[:, h, :].astype(BF16)
        vc = vc_ref[:, h, :].astype(BF16)
        heads = [h * group + g for g in range(group)]
        q = jnp.concatenate([q_ref[:, j * HEAD_DIM:(j + 1) * HEAD_DIM] for j in heads], axis=0).astype(BF16)
        sc = _dot_t(q, kc)
        sn = _dot_t(q, kn.astype(BF16))
        if has_bias:
            sc = sc + bias_ref[h, :t, :lc]
            sn = sn + bias_ref[h, :t, lc:lc + t]
        m = jnp.maximum(jnp.max(sc, axis=-1, keepdims=True), jnp.max(sn, axis=-1, keepdims=True))
        if has_sink:
            sk = jnp.zeros((group * t, 1), F32)
            for g, j in enumerate(heads):
                sk = jnp.where((row >= g * t) & (row < (g + 1) * t), sink_ref[j], sk)
            m = jnp.maximum(m, sk)
        pc = jnp.exp(sc - m)
        pn = jnp.exp(sn - m)
        l = jnp.sum(pc, axis=-1, keepdims=True) + jnp.sum(pn, axis=-1, keepdims=True)
        if has_sink:
            l = l + jnp.exp(sk - m)
        o = (_dot(pc.astype(BF16), vc) + _dot(pn.astype(BF16), vn.astype(BF16))) / l
        for g, j in enumerate(heads):
            o_ref[:, j * HEAD_DIM:(j + 1) * HEAD_DIM] = o[g * t:(g + 1) * t].astype(o_ref.dtype)


def band_sample(q, k_new, v_new, cache_k, cache_v, stack_k, stack_v, layer, *, t, group, bias=None, sink=None):
    _, nb, lc, n_kv, _ = cache_k.shape
    rows = lambda width: pl.BlockSpec((t, width), lambda b: (b, 0))
    cache_spec = pl.BlockSpec((None, None, lc, n_kv, HEAD_DIM), lambda b: (layer, b, 0, 0, 0))
    in_specs = [rows(q.shape[1]), rows(k_new.shape[1]), rows(v_new.shape[1]), cache_spec, cache_spec]
    args = [q, k_new, v_new, cache_k, cache_v]
    if bias is not None:
        assert lc == B_REACH and t <= CHUNK
        in_specs.append(pl.BlockSpec(bias.shape, lambda b: (0, 0, 0)))
        args.append(bias)
    if sink is not None:
        in_specs.append(pl.BlockSpec(memory_space=pltpu.SMEM))
        args.append(sink)
    aliases = {}
    if stack_k is not None:
        aliases = {len(args): 1, len(args) + 1: 2}
        in_specs += [pl.BlockSpec(memory_space=pl.ANY)] * 2
        args += [stack_k, stack_v]
    return pl.pallas_call(
        functools.partial(_band_sample_kernel, group=group, n_kv=n_kv, has_bias=bias is not None,
                          has_sink=sink is not None),
        grid=(nb,),
        in_specs=in_specs,
        out_specs=[rows(q.shape[1]), cache_spec, cache_spec],
        out_shape=[jax.ShapeDtypeStruct(q.shape, BF16),
                   jax.ShapeDtypeStruct(cache_k.shape, F32), jax.ShapeDtypeStruct(cache_v.shape, F32)],
        input_output_aliases=aliases,
        compiler_params=_params(("parallel",)),
        name="band_sample",
    )(*args)


def _tri2():
    j = lax.broadcasted_iota(jnp.int32, (SB_BLOCK, 2 * SB_BLOCK), 0)
    s = lax.broadcasted_iota(jnp.int32, (SB_BLOCK, 2 * SB_BLOCK), 1)
    return jnp.where((s >= SB_BLOCK) | (j > s), 1.0, 0.0).astype(BF16)


def _softplus(z):
    neg_abs = lax.bitcast_convert_type(lax.bitcast_convert_type(z, jnp.int32) | jnp.int32(-2 ** 31), F32)
    return jnp.maximum(z, 0.0) + jnp.log(1.0 + jnp.exp(neg_abs))


def _sb_tile(z, tri2, carry):
    nls = _softplus(z)
    la = _dot(nls.astype(BF16), tri2)
    logw = z - nls - la[:, :SB_BLOCK]
    if carry is None:
        return logw, la[:, SB_BLOCK:]
    return logw - carry, carry + la[:, SB_BLOCK:]


def _causal_tile_mask(rows):
    r = lax.broadcasted_iota(jnp.int32, (rows, SB_BLOCK), 0)
    c = lax.broadcasted_iota(jnp.int32, (rows, SB_BLOCK), 1)
    return c < r


def _sb_block(ref, j):
    return ref[pl.ds(pl.multiple_of(j * SB_BLOCK, SB_BLOCK), SB_BLOCK), :].astype(BF16)


def _sb_older_blocks(q, k_ref, v_ref, tri2, first, acc, carry):
    def cond(state):
        j, _, carry = state
        return (j >= 0) & (jnp.min(carry) < SB_SKIP_LOG)

    def body(state):
        j, acc, carry = state
        logw, carry = _sb_tile(_dot_t(q, _sb_block(k_ref, j)), tri2, carry)
        acc = acc + _dot(jnp.exp(logw).astype(BF16), _sb_block(v_ref, j))
        return j - 1, acc, carry

    _, acc, _ = lax.while_loop(cond, body, (first, acc, carry))
    return acc


def _sb_prompt_kernel(q_ref, k_ref, v_ref, o_ref):
    tq = q_ref.shape[0]
    nsub = tq // SB_BLOCK
    nwin = nsub + SB_LOOKBACK
    pid = pl.program_id(1)
    base = pid * nsub
    is_first = pid == 0
    start = pl.multiple_of(jnp.maximum(base - SB_LOOKBACK, 0) * SB_BLOCK, SB_BLOCK)
    q = q_ref[...]
    kw = k_ref[pl.ds(start, nwin * SB_BLOCK), :]
    vw = v_ref[pl.ds(start, nwin * SB_BLOCK), :]
    kw = jnp.where(is_first, jnp.roll(kw, SB_LOOKBACK * SB_BLOCK, axis=0), kw)
    vw = jnp.where(is_first, jnp.roll(vw, SB_LOOKBACK * SB_BLOCK, axis=0), vw)
    tri2 = _tri2()
    z = _dot_t(q, kw)
    diag = _causal_tile_mask(SB_BLOCK)
    accs, carries = [], []
    for a in range(nsub):
        rows = slice(a * SB_BLOCK, (a + 1) * SB_BLOCK)
        carry = None
        pieces = []
        for b in range(a + SB_LOOKBACK, -1, -1):
            zt = z[rows, b * SB_BLOCK:(b + 1) * SB_BLOCK]
            if b == a + SB_LOOKBACK:
                zt = jnp.where(diag, zt, NEG_INF)
            if b < SB_LOOKBACK:
                zt = jnp.where(is_first, NEG_INF, zt)
            logw, carry = _sb_tile(zt, tri2, carry)
            pieces.insert(0, logw)
        w = jnp.exp(jnp.concatenate(pieces, axis=1)).astype(BF16)
        accs.append(_dot(w, vw[:(a + SB_LOOKBACK + 1) * SB_BLOCK]))
        carries.append(carry)
    acc = jnp.concatenate(accs, axis=0)
    carry = jnp.concatenate(carries, axis=0)
    acc = _sb_older_blocks(q, k_ref, v_ref, tri2, base - SB_LOOKBACK - 1, acc, carry)
    o_ref[...] = acc.astype(o_ref.dtype)


def sb_prompt(q, k, v):
    s = q.shape[0]
    tq = min(SB_QUERY_ROWS, s)
    assert s % tq == 0 and tq % SB_BLOCK == 0 and s >= tq + SB_LOOKBACK * SB_BLOCK
    head_all = pl.BlockSpec((s, HEAD_DIM), lambda h, i: (0, h))
    return pl.pallas_call(
        _sb_prompt_kernel,
        grid=(HC, s // tq),
        in_specs=[pl.BlockSpec((tq, HEAD_DIM), lambda h, i: (i, h)), head_all, head_all],
        out_specs=pl.BlockSpec((tq, HEAD_DIM), lambda h, i: (i, h)),
        out_shape=jax.ShapeDtypeStruct(q.shape, BF16),
        compiler_params=_params(("parallel", "arbitrary")),
        name="sb_prompt",
    )(q, k, v)


def _sb_sample_kernel(q_ref, kn_ref, vn_ref, kc_ref, vc_ref, o_ref, acc_ref, carry_ref):
    s = pl.program_id(1)
    t = q_ref.shape[0]
    n_heads = acc_ref.shape[0]
    tri2 = _tri2()

    def head_q(h):
        return q_ref[:, h * HEAD_DIM:(h + 1) * HEAD_DIM].astype(BF16)

    @pl.when(s == 0)
    def _():
        zeros = jnp.zeros((SB_BLOCK - t, HEAD_DIM), BF16)
        mask = _causal_tile_mask(t)
        for h in range(n_heads):
            cols = slice(h * HEAD_DIM, (h + 1) * HEAD_DIM)
            kb = jnp.concatenate([kn_ref[:, cols].astype(BF16), zeros], axis=0)
            vb = jnp.concatenate([vn_ref[:, cols].astype(BF16), zeros], axis=0)
            logw, carry = _sb_tile(jnp.where(mask, _dot_t(head_q(h), kb), NEG_INF), tri2, None)
            acc_ref[h] = _dot(jnp.exp(logw).astype(BF16), vb)
            carry_ref[h] = carry

    @pl.when(jnp.min(carry_ref[...]) < SB_SKIP_LOG)
    def _():
        for h in range(n_heads):
            kb = kc_ref[:, h, :].astype(BF16)
            vb = vc_ref[:, h, :].astype(BF16)
            logw, carry = _sb_tile(_dot_t(head_q(h), kb), tri2, carry_ref[h])
            acc_ref[h] += _dot(jnp.exp(logw).astype(BF16), vb)
            carry_ref[h] = carry

    @pl.when(s == pl.num_programs(1) - 1)
    def _():
        for h in range(n_heads):
            o_ref[:, h * HEAD_DIM:(h + 1) * HEAD_DIM] = acc_ref[h].astype(o_ref.dtype)


def sb_sample(q, k_new, v_new, cache_k, cache_v, layer, *, t):
    _, nb, past, n_heads, _ = cache_k.shape
    assert past % SB_BLOCK == 0 and t <= SB_BLOCK
    nblk = past // SB_BLOCK
    rows = pl.BlockSpec((t, n_heads * HEAD_DIM), lambda b, s: (b, 0))
    cache_spec = pl.BlockSpec((None, None, SB_BLOCK, n_heads, HEAD_DIM), lambda b, s: (layer, b, nblk - 1 - s, 0, 0))
    return pl.pallas_call(
        _sb_sample_kernel,
        grid=(nb, nblk),
        in_specs=[rows, rows, rows, cache_spec, cache_spec],
        out_specs=rows,
        out_shape=jax.ShapeDtypeStruct(q.shape, BF16),
        scratch_shapes=[pltpu.VMEM((n_heads, t, HEAD_DIM), F32), pltpu.VMEM((n_heads, t, HEAD_DIM), F32)],
        compiler_params=_params(("parallel", "arbitrary")),
        name="sb_sample",
    )(q, k_new, v_new, cache_k, cache_v)


def _rope_table_kernel(inv_ref, cos_ref, sin_ref, *, pos0):
    tm = cos_ref.shape[0]
    pos = lax.broadcasted_iota(jnp.int32, (tm, HEAD_DIM), 0) + (pos0 + pl.program_id(0) * tm)
    lane = lax.broadcasted_iota(jnp.int32, (tm, HEAD_DIM), 1)
    ang = pos.astype(F32) * inv_ref[...]
    sin = jnp.sin(ang)
    cos_ref[...] = jnp.cos(ang)
    sin_ref[...] = jnp.where(lane < HEAD_DIM // 2, -sin, sin)


def _rope_tables(n, pos0):
    half = HEAD_DIM // 2
    inv = ROPE_THETA ** (-jnp.arange(half, dtype=F32) / half)
    inv = jnp.concatenate([inv, inv]).reshape(1, HEAD_DIM)
    tm = min(1024, n)
    assert n % tm == 0
    return pl.pallas_call(
        functools.partial(_rope_table_kernel, pos0=pos0),
        grid=(n // tm,),
        in_specs=[pl.BlockSpec((1, HEAD_DIM), lambda i: (0, 0))],
        out_specs=[pl.BlockSpec((tm, HEAD_DIM), lambda i: (i, 0))] * 2,
        out_shape=[jax.ShapeDtypeStruct((n, HEAD_DIM), F32)] * 2,
        compiler_params=_params(("parallel",)),
        name="rope_tables",
    )(inv)


def kernel(x_prompt, x_sample, cache_a_k, cache_a_v, cache_b_k, cache_b_v, cache_c_k, cache_c_v, norm_mix, w_in_ab, sink_a, rel_bias_b, w_out_ab, w_in_c, w_out_c, norm_ffn, w_gate, w_up, w_down, norm_final):
    bp, seq, d = x_prompt.shape
    nb, t, _ = x_sample.shape
    depth = norm_mix.shape[0]
    past = cache_c_k.shape[2]
    assert bp == 1 and seq >= B_REACH
    keep_a, keep_b = min(A_REACH, seq), min(B_REACH, seq)
    wc = HC * HEAD_DIM

    yp = x_prompt.reshape(seq, d)
    ys = x_sample.reshape(nb * t, d)
    cos_p, sin_p = _rope_tables(seq, 0)
    cos_s, sin_s = _rope_tables(t, past)
    cos_s, sin_s = jnp.tile(cos_s, (nb, 1)), jnp.tile(sin_s, (nb, 1))

    outs = {name: [] for name in ("pa_k", "pa_v", "pb_k", "pb_v", "sc_k", "sc_v")}
    n_c = cache_c_k.shape[0]
    sa_k = sa_v = sb_k = sb_v = pc_k = pc_v = None
    hp, qp = norm_inputs(yp)
    hs, qs_ = norm_inputs(ys)

    def weight_jobs(layer):
        i = layer // 2
        if layer % 2 == 0:
            mixer = [CastJob(w_in_ab, i, norm_mix[layer], AB_TILE), CastJob(w_out_ab, i, None, OUT_TILE)]
        else:
            mixer = [CastJob(w_in_c, i, norm_mix[layer], None), CastJob(w_out_c, i, None, None)]
        return mixer + [CastJob(w_gate, layer, norm_ffn[layer], FF_TILE), CastJob(w_up, layer, norm_ffn[layer], FF_TILE)]

    w_in, w_out, wg, wu = [cast_bf16(*job) for job in weight_jobs(0)]
    for layer in range(depth):
        i = layer // 2
        if layer % 2 == 0:
            bias = bias_prep(rel_bias_b[i])
            proj = matmul_ab(hp, qp, w_in, cos_p, sin_p, BF16)
            oa = band_prompt(proj, q_col=0, k_col=AB_SPLITS[0], v_col=AB_SPLITS[1], n_kv=KV_A, group=G_A,
                             halo=A_REACH, n_band=A_BAND_CHUNKS, sink=sink_a[i])
            ob = band_prompt(proj, q_col=AB_SPLITS[2], k_col=AB_SPLITS[3], v_col=AB_SPLITS[4], n_kv=HB, group=1,
                             halo=B_REACH, n_band=B_BAND_CHUNKS, bias=bias)
            tail = matmul_ab(hp[seq - keep_b:], qp[seq - keep_b:], w_in, cos_p[seq - keep_b:], sin_p[seq - keep_b:], F32)
            yp, hp, qp = matmul2_res(oa, ob, w_out, yp)
            outs["pa_k"].append(tail[keep_b - keep_a:, AB_SPLITS[0]:AB_SPLITS[1]].reshape(1, keep_a, KV_A, HEAD_DIM))
            outs["pa_v"].append(tail[keep_b - keep_a:, AB_SPLITS[1]:AB_SPLITS[2]].reshape(1, keep_a, KV_A, HEAD_DIM))
            outs["pb_k"].append(tail[:, AB_SPLITS[3]:AB_SPLITS[4]].reshape(1, keep_b, HB, HEAD_DIM))
            outs["pb_v"].append(tail[:, AB_SPLITS[4]:].reshape(1, keep_b, HB, HEAD_DIM))
            projs = matmul_ab(hs, qs_, w_in, cos_s, sin_s, F32)
            qa, ka, va, qb, kb, vb = jnp.split(projs, AB_SPLITS, axis=1)
            oa, sa_k, sa_v = band_sample(qa, ka, va, cache_a_k, cache_a_v, sa_k, sa_v, i, t=t, group=G_A,
                                         sink=sink_a[i])
            ob, sb_k, sb_v = band_sample(qb, kb, vb, cache_b_k, cache_b_v, sb_k, sb_v, i, t=t, group=1, bias=bias)
            ys, hs, qs_ = matmul2_res(oa, ob, w_out, ys)
        else:
            q = matmul(hp, qp, w_in, BF16, scale=QK_SCALE, col0=0, n=wc)
            pc_k, k16 = matmul_heads(hp, qp, w_in, pc_k, i, n_c, col0=wc, n=wc)
            pc_v, v16 = matmul_heads(hp, qp, w_in, pc_v, i, n_c, col0=2 * wc, n=wc)
            att = sb_prompt(q, k16, v16)
            yp, hp, qp = matmul_res(att, w_out, yp, tm=1024, tn=512)
            q = matmul(hs, qs_, w_in, F32, scale=QK_SCALE, col0=0, n=wc)
            ks = matmul(hs, qs_, w_in, F32, col0=wc, n=wc)
            vs = matmul(hs, qs_, w_in, F32, col0=2 * wc, n=wc)
            att = sb_sample(q, ks, vs, cache_c_k, cache_c_v, i, t=t)
            ys, hs, qs_ = matmul_res(att, w_out, ys, tm=1024, tn=512)
            outs["sc_k"].append(ks.reshape(nb, t, HC, HEAD_DIM))
            outs["sc_v"].append(vs.reshape(nb, t, HC, HEAD_DIM))
        jobs = [CastJob(w_down, layer, None, OUT_TILE)] + (weight_jobs(layer + 1) if layer + 1 < depth else [])
        act_p, casts = gateup(hp, qp, wg, wu, jobs)
        act_s, _ = gateup(hs, qs_, wg, wu)
        wd = casts[0]
        yp, hp, qp = matmul_res(act_p, wd, yp)
        ys, hs, qs_ = matmul_res(act_s, wd, ys)
        if layer + 1 < depth:
            w_in, w_out, wg, wu = casts[1:]

    y_prompt = rmsnorm(yp, norm_final, F32).reshape(1, seq, d)
    y_sample = rmsnorm(ys, norm_final, F32).reshape(nb, t, d)
    st = {name: jnp.stack(v) for name, v in outs.items()}
    pc_k = pc_k.reshape(n_c, 1, seq, HC, HEAD_DIM)
    pc_v = pc_v.reshape(n_c, 1, seq, HC, HEAD_DIM)
    return (y_prompt, y_sample, st["pa_k"], st["pa_v"], st["pb_k"], st["pb_v"], pc_k, pc_v,
            sa_k, sa_v, sb_k, sb_v, st["sc_k"], st["sc_v"])
```

```python
import functools
from typing import NamedTuple, Optional

import jax
import jax.numpy as jnp
from jax import lax
from jax.experimental import pallas as pl
from jax.experimental.pallas import tpu as pltpu

HEAD_DIM = 128
CHUNK = 64
HA = 16
KV_A = 4
G_A = HA // KV_A
HB = 16
HC = 16
A_BAND_CHUNKS = 3
B_BAND_CHUNKS = 9
A_REACH = (A_BAND_CHUNKS - 1) * CHUNK
B_REACH = (B_BAND_CHUNKS - 1) * CHUNK
REL_CLIP = 256
N_REL = 2 * REL_CLIP + 1
SB_BLOCK = 128
ROPE_THETA = 10000.0
RMS_EPS = 1e-6
NEG_INF = -1e30
QA_W = HA * HEAD_DIM
KA_W = KV_A * HEAD_DIM
QB_W = HB * HEAD_DIM
AB_SPLITS = (QA_W, QA_W + KA_W, QA_W + 2 * KA_W, QA_W + 2 * KA_W + QB_W, QA_W + 2 * KA_W + 2 * QB_W)
QK_SCALE = HEAD_DIM ** -0.5

SB_QUERY_ROWS = 512
SB_LOOKBACK = 2
SB_SKIP_LOG = 106.0
BF16_ROWS = 16
AB_TILE = 512
OUT_TILE = 512
FF_TILE = 256
SUB = 128
BAND_QB = 2048
VMEM_LIMIT_MB = 56

F32 = jnp.float32
BF16 = jnp.bfloat16


def _params(semantics):
    return pltpu.CompilerParams(dimension_semantics=semantics, vmem_limit_bytes=VMEM_LIMIT_MB << 20)


def _dot(a, b):
    return jnp.dot(a, b, preferred_element_type=F32)


def _dot_t(a, b):
    return lax.dot_general(a, b, (((1,), (1,)), ((), ())), preferred_element_type=F32)


def _rmsnorm_kernel(x_ref, g_ref, o_ref):
    x = x_ref[...]
    ms = jnp.mean(x * x, axis=-1, keepdims=True)
    o_ref[...] = (x * lax.rsqrt(ms + RMS_EPS) * g_ref[...]).astype(o_ref.dtype)


def rmsnorm(x, g, out_dtype):
    m, d = x.shape
    tm = min(256, m)
    return pl.pallas_call(
        _rmsnorm_kernel,
        grid=(m // tm,),
        in_specs=[pl.BlockSpec((tm, d), lambda i: (i, 0)), pl.BlockSpec((1, d), lambda i: (0, 0))],
        out_specs=pl.BlockSpec((tm, d), lambda i: (i, 0)),
        out_shape=jax.ShapeDtypeStruct((m, d), out_dtype),
        compiler_params=_params(("parallel",)),
        name="rmsnorm",
    )(x, g.reshape(1, d))


def _lane_partial_ssq(y):
    sq = y * y
    part = sq[:, :HEAD_DIM]
    for c in range(1, y.shape[1] // HEAD_DIM):
        part = part + sq[:, c * HEAD_DIM:(c + 1) * HEAD_DIM]
    return part


def _row_rstd(ssq_ref, d):
    return lax.rsqrt(jnp.sum(ssq_ref[...], axis=-1, keepdims=True) / d + RMS_EPS)


def _norm_inputs_kernel(x_ref, o16_ref, ssq_ref):
    x = x_ref[...]
    o16_ref[...] = x.astype(o16_ref.dtype)
    ssq_ref[...] = _lane_partial_ssq(x)


def norm_inputs(x):
    m, d = x.shape
    tm = min(256, m)
    return pl.pallas_call(
        _norm_inputs_kernel,
        grid=(m // tm,),
        in_specs=[pl.BlockSpec((tm, d), lambda i: (i, 0))],
        out_specs=[pl.BlockSpec((tm, d), lambda i: (i, 0)), pl.BlockSpec((tm, HEAD_DIM), lambda i: (i, 0))],
        out_shape=[jax.ShapeDtypeStruct((m, d), BF16), jax.ShapeDtypeStruct((m, HEAD_DIM), F32)],
        compiler_params=_params(("parallel",)),
        name="norm_inputs",
    )(x)


def _cast_block(x_ref, gain_ref, o_ref):
    x = x_ref[...]
    if gain_ref is not None:
        x = x * gain_ref[...]
    x = x.astype(o_ref.dtype)
    if len(o_ref.shape) == 2:
        o_ref[...] = x
    else:
        ct = o_ref.shape[2]
        for c in range(o_ref.shape[0]):
            o_ref[c] = x[:, c * ct:(c + 1) * ct]


def _cast_kernel(x_ref, *rest):
    _cast_block(x_ref, rest[0] if len(rest) == 2 else None, rest[-1])


class CastJob(NamedTuple):
    stack: jax.Array
    layer: int
    gain: Optional[jax.Array]
    col_tile: Optional[int]


def _cast_job_specs(job, n_steps, step_of):
    _, k, n = job.stack.shape
    tr = next(r for r in range(BF16_ROWS, k + 1, BF16_ROWS) if k % r == 0 and k // r <= n_steps)
    last = k // tr - 1
    blk = lambda i, j: jnp.minimum(step_of(i, j), last)
    layer = job.layer
    in_specs = [pl.BlockSpec((None, tr, n), lambda i, j: (layer, blk(i, j), 0))]
    args = [job.stack]
    if job.gain is not None:
        in_specs.append(pl.BlockSpec((tr, 1), lambda i, j: (blk(i, j), 0)))
        args.append(job.gain.reshape(k, 1))
    if job.col_tile is None:
        out_spec = pl.BlockSpec((tr, n), lambda i, j: (blk(i, j), 0))
        out_shape = jax.ShapeDtypeStruct((k, n), BF16)
    else:
        ct = job.col_tile
        out_spec = pl.BlockSpec((n // ct, tr, ct), lambda i, j: (0, blk(i, j), 0))
        out_shape = jax.ShapeDtypeStruct((n // ct, k, ct), BF16)
    return in_specs, args, out_spec, out_shape, last + 1


def cast_bf16(w_stack, layer, gain=None, col_tile=None):
    _, k, n = w_stack.shape
    tr = min(256, k)
    assert k % tr == 0
    in_specs = [pl.BlockSpec((None, tr, n), lambda i: (layer, i, 0))]
    args = [w_stack]
    if gain is not None:
        in_specs.append(pl.BlockSpec((tr, 1), lambda i: (i, 0)))
        args.append(gain.reshape(k, 1))
    if col_tile is None:
        out_spec = pl.BlockSpec((tr, n), lambda i: (i, 0))
        out_shape = jax.ShapeDtypeStruct((k, n), BF16)
    else:
        assert n % col_tile == 0
        out_spec = pl.BlockSpec((n // col_tile, tr, col_tile), lambda i: (0, i, 0))
        out_shape = jax.ShapeDtypeStruct((n // col_tile, k, col_tile), BF16)
    return pl.pallas_call(
        _cast_kernel,
        grid=(k // tr,),
        in_specs=in_specs,
        out_specs=out_spec,
        out_shape=out_shape,
        compiler_params=_params(("parallel",)),
        name="cast_bf16",
    )(*args)


def _w_spec(w, k, tn, c0=0):
    if w.ndim == 2:
        return pl.BlockSpec((k, tn), lambda i, j: (0, c0 + j))
    assert w.shape[1:] == (k, tn), (w.shape, k, tn)
    return pl.BlockSpec((None, k, tn), lambda i, j: (c0 + j, 0, 0))


def _w_cols(w):
    return w.shape[1] if w.ndim == 2 else w.shape[0] * w.shape[2]


def _mm_kernel(a_ref, ssq_ref, w_ref, o_ref, *, scale):
    acc = _dot(a_ref[...], w_ref[...]) * (_row_rstd(ssq_ref, a_ref.shape[1]) * scale)
    o_ref[...] = acc.astype(o_ref.dtype)


def _mm_heads_kernel(a_ref, ssq_ref, w_ref, *rest):
    o32_ref, o16_ref = rest[-2:]
    acc = _dot(a_ref[...], w_ref[...]) * _row_rstd(ssq_ref, a_ref.shape[1])
    for h in range(o32_ref.shape[1]):
        o32_ref[:, h, :] = acc[:, h * HEAD_DIM:(h + 1) * HEAD_DIM]
    o16_ref[...] = acc.astype(o16_ref.dtype)


def _emit_residual(y, o_ref, o16_ref, ssq_ref):
    o_ref[...] = y
    o16_ref[...] = y.astype(o16_ref.dtype)
    part = _lane_partial_ssq(y)
    first = pl.program_id(1) == 0

    @pl.when(first)
    def _():
        ssq_ref[...] = part

    @pl.when(jnp.logical_not(first))
    def _():
        ssq_ref[...] += part


def _mm_res_kernel(a_ref, w_ref, r_ref, o_ref, o16_ref, ssq_ref):
    _emit_residual(r_ref[...] + _dot(a_ref[...], w_ref[...]), o_ref, o16_ref, ssq_ref)


def _mm2_res_kernel(a1_ref, a2_ref, w_ref, r_ref, o_ref, o16_ref, ssq_ref):
    k1 = a1_ref.shape[1]
    acc = _dot(a1_ref[...], w_ref[:k1, :]) + _dot(a2_ref[...], w_ref[k1:, :])
    _emit_residual(r_ref[...] + acc, o_ref, o16_ref, ssq_ref)


def _gateup_kernel(a_ref, ssq_ref, wg_ref, wu_ref, *rest, job_has_gain, job_blocks):
    n_jobs = len(job_has_gain)
    o_ref = rest[len(rest) - n_jobs - 1]
    a = a_ref[...]
    rstd = _row_rstd(ssq_ref, a_ref.shape[1])
    g = _dot(a, wg_ref[...]) * rstd
    u = _dot(a, wu_ref[...]) * rstd
    o_ref[...] = (g / (1.0 + jnp.exp(-g)) * u).astype(o_ref.dtype)
    step = pl.program_id(0) * pl.num_programs(1) + pl.program_id(1)
    pos = 0
    for job, (has_gain, n_blocks) in enumerate(zip(job_has_gain, job_blocks)):
        x_ref = rest[pos]
        gain_ref = rest[pos + 1] if has_gain else None
        pos += 2 if has_gain else 1
        pl.when(step < n_blocks)(functools.partial(_cast_block, x_ref, gain_ref, rest[len(rest) - n_jobs + job]))


def _mm_ab_kernel(a_ref, ssq_ref, w_ref, cos_ref, sin_ref, o_ref, *, tn):
    col0 = pl.program_id(1) * tn
    acc = _dot(a_ref[...], w_ref[...]) * _row_rstd(ssq_ref, a_ref.shape[1])
    is_rope = col0 < AB_SPLITS[1]
    is_q = (col0 < AB_SPLITS[0]) | ((col0 >= AB_SPLITS[2]) & (col0 < AB_SPLITS[3]))
    s = jnp.where(is_q, QK_SCALE, 1.0).astype(F32)

    @pl.when(is_rope)
    def _():
        cos = cos_ref[...]
        sin = sin_ref[...]
        for g in range(tn // HEAD_DIM):
            blk = acc[:, g * HEAD_DIM:(g + 1) * HEAD_DIM]
            rot = blk * cos + pltpu.roll(blk, HEAD_DIM // 2, axis=1) * sin
            o_ref[:, g * HEAD_DIM:(g + 1) * HEAD_DIM] = (rot * s).astype(o_ref.dtype)

    @pl.when(jnp.logical_not(is_rope))
    def _():
        o_ref[...] = (acc * s).astype(o_ref.dtype)


def _tiles(m, n, tm, tn):
    tm = min(tm, m)
    tn = min(tn, n)
    assert m % tm == 0 and n % tn == 0, (m, n, tm, tn)
    return tm, tn


def _ssq_spec(tm):
    return pl.BlockSpec((tm, HEAD_DIM), lambda i, j: (i, 0))


def _residual_out_specs(tm, tn):
    return [pl.BlockSpec((tm, tn), lambda i, j: (i, j)), pl.BlockSpec((tm, tn), lambda i, j: (i, j)), _ssq_spec(tm)]


def _residual_out_shapes(m, n):
    return [jax.ShapeDtypeStruct((m, n), F32), jax.ShapeDtypeStruct((m, n), BF16),
            jax.ShapeDtypeStruct((m, HEAD_DIM), F32)]


def matmul(a, ssq, w, out_dtype, *, scale=1.0, col0=0, n=None, tm=1024, tn=512):
    m, k = a.shape
    n = w.shape[1] - col0 if n is None else n
    tm, tn = _tiles(m, n, tm, tn)
    assert col0 % tn == 0
    c0 = col0 // tn
    return pl.pallas_call(
        functools.partial(_mm_kernel, scale=scale),
        grid=(m // tm, n // tn),
        in_specs=[pl.BlockSpec((tm, k), lambda i, j: (i, 0)), _ssq_spec(tm),
                  pl.BlockSpec((k, tn), lambda i, j: (0, c0 + j))],
        out_specs=pl.BlockSpec((tm, tn), lambda i, j: (i, j)),
        out_shape=jax.ShapeDtypeStruct((m, n), out_dtype),
        compiler_params=_params(("parallel", "arbitrary")),
        name="matmul",
    )(a, ssq, w)


def matmul_heads(a, ssq, w, stack, layer, n_layers, *, col0, n, tm=1024, tn=1024):
    m, k = a.shape
    tm, tn = _tiles(m, n, tm, tn)
    assert col0 % tn == 0 and tn % (8 * HEAD_DIM) == 0
    c0 = col0 // tn
    hpt = tn // HEAD_DIM
    in_specs = [pl.BlockSpec((tm, k), lambda i, j: (i, 0)), _ssq_spec(tm),
                pl.BlockSpec((k, tn), lambda i, j: (0, c0 + j))]
    args = [a, ssq, w]
    if stack is not None:
        in_specs.append(pl.BlockSpec(memory_space=pl.ANY))
        args.append(stack)
    return pl.pallas_call(
        _mm_heads_kernel,
        grid=(m // tm, n // tn),
        in_specs=in_specs,
        out_specs=[pl.BlockSpec((None, tm, hpt, HEAD_DIM), lambda i, j: (layer, i, j, 0)),
                   pl.BlockSpec((tm, tn), lambda i, j: (i, j))],
        out_shape=[jax.ShapeDtypeStruct((n_layers, m, n // HEAD_DIM, HEAD_DIM), F32), jax.ShapeDtypeStruct((m, n), BF16)],
        input_output_aliases={} if stack is None else {3: 0},
        compiler_params=_params(("parallel", "arbitrary")),
        name="matmul_heads",
    )(*args)


def matmul_res(a, w, res, *, tm=512, tn=OUT_TILE):
    m, k = a.shape
    n = _w_cols(w)
    tm, tn = _tiles(m, n, tm, tn)
    return pl.pallas_call(
        _mm_res_kernel,
        grid=(m // tm, n // tn),
        in_specs=[pl.BlockSpec((tm, k), lambda i, j: (i, 0)), _w_spec(w, k, tn),
                  pl.BlockSpec((tm, tn), lambda i, j: (i, j))],
        out_specs=_residual_out_specs(tm, tn),
        out_shape=_residual_out_shapes(m, n),
        compiler_params=_params(("parallel", "arbitrary")),
        name="matmul_res",
    )(a, w, res)


def matmul2_res(a1, a2, w, res, *, tm=1024, tn=OUT_TILE):
    m, k1 = a1.shape
    k2 = a2.shape[1]
    n = _w_cols(w)
    tm, tn = _tiles(m, n, tm, tn)
    return pl.pallas_call(
        _mm2_res_kernel,
        grid=(m // tm, n // tn),
        in_specs=[pl.BlockSpec((tm, k1), lambda i, j: (i, 0)), pl.BlockSpec((tm, k2), lambda i, j: (i, 0)),
                  _w_spec(w, k1 + k2, tn), pl.BlockSpec((tm, tn), lambda i, j: (i, j))],
        out_specs=_residual_out_specs(tm, tn),
        out_shape=_residual_out_shapes(m, n),
        compiler_params=_params(("parallel", "arbitrary")),
        name="matmul2_res",
    )(a1, a2, w, res)


def gateup(a, ssq, wg, wu, cast_jobs=(), *, tm=1024, tn=FF_TILE):
    m, k = a.shape
    n = _w_cols(wg)
    tm, tn = _tiles(m, n, tm, tn)
    grid = (m // tm, n // tn)
    in_specs = [pl.BlockSpec((tm, k), lambda i, j: (i, 0)), _ssq_spec(tm), _w_spec(wg, k, tn), _w_spec(wu, k, tn)]
    args = [a, ssq, wg, wu]
    out_specs = [pl.BlockSpec((tm, tn), lambda i, j: (i, j))]
    out_shape = [jax.ShapeDtypeStruct((m, n), BF16)]
    job_blocks = []
    for job in cast_jobs:
        j_in, j_args, j_out, j_shape, n_blocks = _cast_job_specs(job, grid[0] * grid[1], lambda i, j: i * grid[1] + j)
        in_specs += j_in
        args += j_args
        out_specs.append(j_out)
        out_shape.append(j_shape)
        job_blocks.append(n_blocks)
    outs = pl.pallas_call(
        functools.partial(_gateup_kernel, job_has_gain=tuple(job.gain is not None for job in cast_jobs),
                          job_blocks=tuple(job_blocks)),
        grid=grid,
        in_specs=in_specs,
        out_specs=out_specs,
        out_shape=out_shape,
        compiler_params=_params(("parallel", "arbitrary")),
        name="gateup",
    )(*args)
    return outs[0], list(outs[1:])


def matmul_ab(a, ssq, w, cos, sin, out_dtype, *, tm=2048):
    m, k = a.shape
    n = _w_cols(w)
    tn = AB_TILE
    assert all(s % tn == 0 for s in AB_SPLITS)
    tm, tn = _tiles(m, n, tm, tn)
    return pl.pallas_call(
        functools.partial(_mm_ab_kernel, tn=tn),
        grid=(m // tm, n // tn),
        in_specs=[pl.BlockSpec((tm, k), lambda i, j: (i, 0)), _ssq_spec(tm), _w_spec(w, k, tn),
                  pl.BlockSpec((tm, HEAD_DIM), lambda i, j: (i, 0)), pl.BlockSpec((tm, HEAD_DIM), lambda i, j: (i, 0))],
        out_specs=pl.BlockSpec((tm, tn), lambda i, j: (i, j)),
        out_shape=jax.ShapeDtypeStruct((m, n), out_dtype),
        compiler_params=_params(("parallel", "arbitrary")),
        name="matmul_ab",
    )(a, ssq, w, cos, sin)


def _band_mask(rows, halo, n_band):
    r = lax.broadcasted_iota(jnp.int32, (rows, halo + SUB), 0)
    c = lax.broadcasted_iota(jnp.int32, (rows, halo + SUB), 1)
    rc = (r + halo) >> 6
    cc = c >> 6
    return (cc <= rc) & (cc > rc - n_band)


def _bias_prep_kernel(tab_ref, o_ref):
    h = pl.program_id(0)
    width = B_REACH + SUB
    ulen = width + SUB
    n = lax.broadcasted_iota(jnp.int32, (8, ulen), 1)
    idx = jnp.clip(B_REACH + SUB - 1 - n, -REL_CLIP, REL_CLIP) + REL_CLIP

    def body(m, u):
        return jnp.where(idx == m, tab_ref[h, m], u)

    u = lax.fori_loop(0, N_REL, body, jnp.zeros((8, ulen), F32))
    x = jnp.broadcast_to(u[0:1, :], (SUB, ulen))
    x = pltpu.roll(x, ulen - (SUB - 1), axis=1, stride=1, stride_axis=0)
    o_ref[0] = jnp.where(_band_mask(SUB, B_REACH, B_BAND_CHUNKS), x[:, :width], NEG_INF)


def bias_prep(table):
    return pl.pallas_call(
        _bias_prep_kernel,
        grid=(HB,),
        in_specs=[pl.BlockSpec(memory_space=pltpu.SMEM)],
        out_specs=pl.BlockSpec((1, SUB, B_REACH + SUB), lambda h: (h, 0, 0)),
        out_shape=jax.ShapeDtypeStruct((HB, SUB, B_REACH + SUB), F32),
        compiler_params=_params(("arbitrary",)),
        name="bias_prep",
    )(table)


def _band_prompt_kernel(*refs, group, halo, n_band, has_bias, has_sink):
    q_ref, km_ref, kh_ref, vm_ref, vh_ref = refs[:5]
    rest = list(refs[5:])
    bias_ref = rest.pop(0) if has_bias else None
    sink_ref = rest.pop(0) if has_sink else None
    o_ref = rest.pop(0)
    h = pl.program_id(0)
    i = pl.program_id(1)
    qb = q_ref.shape[0]
    width = halo + SUB
    k = jnp.concatenate([kh_ref[...], km_ref[...]], axis=0)
    v = jnp.concatenate([vh_ref[...], vm_ref[...]], axis=0)
    c = lax.broadcasted_iota(jnp.int32, (SUB, width), 1)
    if has_bias:
        bias = bias_ref[0]
    else:
        bias = jnp.where(_band_mask(SUB, halo, n_band), 0.0, NEG_INF).astype(F32)
    for sb in range(qb // SUB):
        kw = k[sb * SUB:sb * SUB + width]
        vw = v[sb * SUB:sb * SUB + width]
        first_valid = halo - i * qb - sb * SUB
        for g in range(group):
            q = q_ref[sb * SUB:(sb + 1) * SUB, g * HEAD_DIM:(g + 1) * HEAD_DIM]
            s = _dot_t(q, kw) + bias
            s = jnp.where(c >= first_valid, s, NEG_INF)
            m = jnp.max(s, axis=-1, keepdims=True)
            if has_sink:
                sk = sink_ref[h * group + g]
                m = jnp.maximum(m, sk)
            p = jnp.exp(s - m)
            l = jnp.sum(p, axis=-1, keepdims=True)
            if has_sink:
                l = l + jnp.exp(sk - m)
            o = _dot(p.astype(BF16), vw) / l
            o_ref[sb * SUB:(sb + 1) * SUB, g * HEAD_DIM:(g + 1) * HEAD_DIM] = o.astype(o_ref.dtype)


def band_prompt(proj, *, q_col, k_col, v_col, n_kv, group, halo, n_band, bias=None, sink=None):
    s = proj.shape[0]
    qb = BAND_QB
    assert s % qb == 0 and qb % halo == 0 and halo % CHUNK == 0
    qw = group * HEAD_DIM
    hpb = qb // halo
    qc, kc, vc = q_col // qw, k_col // HEAD_DIM, v_col // HEAD_DIM
    main = lambda c0: pl.BlockSpec((qb, HEAD_DIM), lambda h, i: (i, c0 + h))
    halo_spec = lambda c0: pl.BlockSpec((halo, HEAD_DIM), lambda h, i: (jnp.maximum(i * hpb - 1, 0), c0 + h))
    in_specs = [pl.BlockSpec((qb, qw), lambda h, i: (i, qc + h)), main(kc), halo_spec(kc), main(vc), halo_spec(vc)]
    args = [proj, proj, proj, proj, proj]
    if bias is not None:
        in_specs.append(pl.BlockSpec((1, SUB, halo + SUB), lambda h, i: (h, 0, 0)))
        args.append(bias)
    if sink is not None:
        in_specs.append(pl.BlockSpec(memory_space=pltpu.SMEM))
        args.append(sink)
    return pl.pallas_call(
        functools.partial(_band_prompt_kernel, group=group, halo=halo, n_band=n_band,
                          has_bias=bias is not None, has_sink=sink is not None),
        grid=(n_kv, s // qb),
        in_specs=in_specs,
        out_specs=pl.BlockSpec((qb, qw), lambda h, i: (i, h)),
        out_shape=jax.ShapeDtypeStruct((s, n_kv * qw), BF16),
        compiler_params=_params(("parallel", "arbitrary")),
        name="band_prompt",
    )(*args)


def _band_sample_kernel(*refs, group, n_kv, has_bias, has_sink):
    q_ref, kn_ref, vn_ref, kc_ref, vc_ref = refs[:5]
    rest = list(refs[5:])
    bias_ref = rest.pop(0) if has_bias else None
    sink_ref = rest.pop(0) if has_sink else None
    o_ref, ko_ref, vo_ref = rest[-3:]
    t = q_ref.shape[0]
    lc = kc_ref.shape[0]
    ko_ref[:lc - t] = kc_ref[t:]
    vo_ref[:lc - t] = vc_ref[t:]
    if has_sink:
        row = lax.broadcasted_iota(jnp.int32, (group * t, 1), 0)
    for h in range(n_kv):
        cols = slice(h * HEAD_DIM, (h + 1) * HEAD_DIM)
        kn = kn_ref[:, cols]
        vn = vn_ref[:, cols]
        ko_ref[lc - t:, h, :] = kn
        vo_ref[lc - t:, h, :] = vn
        kc = kc_ref[:, h, :].astype(BF16)
        vc = vc_ref[:, h, :].astype(BF16)
        heads = [h * group + g for g in range(group)]
        q = jnp.concatenate([q_ref[:, j * HEAD_DIM:(j + 1) * HEAD_DIM] for j in heads], axis=0).astype(BF16)
        sc = _dot_t(q, kc)
        sn = _dot_t(q, kn.astype(BF16))
        if has_bias:
            sc = sc + bias_ref[h, :t, :lc]
            sn = sn + bias_ref[h, :t, lc:lc + t]
        m = jnp.maximum(jnp.max(sc, axis=-1, keepdims=True), jnp.max(sn, axis=-1, keepdims=True))
        if has_sink:
            sk = jnp.zeros((group * t, 1), F32)
            for g, j in enumerate(heads):
                sk = jnp.where((row >= g * t) & (row < (g + 1) * t), sink_ref[j], sk)
            m = jnp.maximum(m, sk)
        pc = jnp.exp(sc - m)
        pn = jnp.exp(sn - m)
        l = jnp.sum(pc, axis=-1, keepdims=True) + jnp.sum(pn, axis=-1, keepdims=True)
        if has_sink:
            l = l + jnp.exp(sk - m)
        o = (_dot(pc.astype(BF16), vc) + _dot(pn.astype(BF16), vn.astype(BF16))) / l
        for g, j in enumerate(heads):
            o_ref[:, j * HEAD_DIM:(j + 1) * HEAD_DIM] = o[g * t:(g + 1) * t].astype(o_ref.dtype)


def band_sample(q, k_new, v_new, cache_k, cache_v, stack_k, stack_v, layer, *, t, group, bias=None, sink=None):
    _, nb, lc, n_kv, _ = cache_k.shape
    rows = lambda width: pl.BlockSpec((t, width), lambda b: (b, 0))
    cache_spec = pl.BlockSpec((None, None, lc, n_kv, HEAD_DIM), lambda b: (layer, b, 0, 0, 0))
    in_specs = [rows(q.shape[1]), rows(k_new.shape[1]), rows(v_new.shape[1]), cache_spec, cache_spec]
    args = [q, k_new, v_new, cache_k, cache_v]
    if bias is not None:
        assert lc == B_REACH and t <= CHUNK
        in_specs.append(pl.BlockSpec(bias.shape, lambda b: (0, 0, 0)))
        args.append(bias)
    if sink is not None:
        in_specs.append(pl.BlockSpec(memory_space=pltpu.SMEM))
        args.append(sink)
    aliases = {}
    if stack_k is not None:
        aliases = {len(args): 1, len(args) + 1: 2}
        in_specs += [pl.BlockSpec(memory_space=pl.ANY)] * 2
        args += [stack_k, stack_v]
    return pl.pallas_call(
        functools.partial(_band_sample_kernel, group=group, n_kv=n_kv, has_bias=bias is not None,
                          has_sink=sink is not None),
        grid=(nb,),
        in_specs=in_specs,
        out_specs=[rows(q.shape[1]), cache_spec, cache_spec],
        out_shape=[jax.ShapeDtypeStruct(q.shape, BF16),
                   jax.ShapeDtypeStruct(cache_k.shape, F32), jax.ShapeDtypeStruct(cache_v.shape, F32)],
        input_output_aliases=aliases,
        compiler_params=_params(("parallel",)),
        name="band_sample",
    )(*args)


def _tri2():
    j = lax.broadcasted_iota(jnp.int32, (SB_BLOCK, 2 * SB_BLOCK), 0)
    s = lax.broadcasted_iota(jnp.int32, (SB_BLOCK, 2 * SB_BLOCK), 1)
    return jnp.where((s >= SB_BLOCK) | (j > s), 1.0, 0.0).astype(BF16)


def _softplus(z):
    neg_abs = lax.bitcast_convert_type(lax.bitcast_convert_type(z, jnp.int32) | jnp.int32(-2 ** 31), F32)
    return jnp.maximum(z, 0.0) + jnp.log(1.0 + jnp.exp(neg_abs))


def _sb_tile(z, tri2, carry):
    nls = _softplus(z)
    la = _dot(nls.astype(BF16), tri2)
    logw = z - nls - la[:, :SB_BLOCK]
    if carry is None:
        return logw, la[:, SB_BLOCK:]
    return logw - carry, carry + la[:, SB_BLOCK:]


def _causal_tile_mask(rows):
    r = lax.broadcasted_iota(jnp.int32, (rows, SB_BLOCK), 0)
    c = lax.broadcasted_iota(jnp.int32, (rows, SB_BLOCK), 1)
    return c < r


def _sb_block(ref, j):
    return ref[pl.ds(pl.multiple_of(j * SB_BLOCK, SB_BLOCK), SB_BLOCK), :].astype(BF16)


def _sb_older_blocks(q, k_ref, v_ref, tri2, first, acc, carry):
    def cond(state):
        j, _, carry = state
        return (j >= 0) & (jnp.min(carry) < SB_SKIP_LOG)

    def body(state):
        j, acc, carry = state
        logw, carry = _sb_tile(_dot_t(q, _sb_block(k_ref, j)), tri2, carry)
        acc = acc + _dot(jnp.exp(logw).astype(BF16), _sb_block(v_ref, j))
        return j - 1, acc, carry

    _, acc, _ = lax.while_loop(cond, body, (first, acc, carry))
    return acc


def _sb_prompt_kernel(q_ref, k_ref, v_ref, o_ref):
    tq = q_ref.shape[0]
    nsub = tq // SB_BLOCK
    nwin = nsub + SB_LOOKBACK
    pid = pl.program_id(1)
    base = pid * nsub
    is_first = pid == 0
    start = pl.multiple_of(jnp.maximum(base - SB_LOOKBACK, 0) * SB_BLOCK, SB_BLOCK)
    q = q_ref[...]
    kw = k_ref[pl.ds(start, nwin * SB_BLOCK), :]
    vw = v_ref[pl.ds(start, nwin * SB_BLOCK), :]
    kw = jnp.where(is_first, jnp.roll(kw, SB_LOOKBACK * SB_BLOCK, axis=0), kw)
    vw = jnp.where(is_first, jnp.roll(vw, SB_LOOKBACK * SB_BLOCK, axis=0), vw)
    tri2 = _tri2()
    z = _dot_t(q, kw)
    diag = _causal_tile_mask(SB_BLOCK)
    accs, carries = [], []
    for a in range(nsub):
        rows = slice(a * SB_BLOCK, (a + 1) * SB_BLOCK)
        carry = None
        pieces = []
        for b in range(a + SB_LOOKBACK, -1, -1):
            zt = z[rows, b * SB_BLOCK:(b + 1) * SB_BLOCK]
            if b == a + SB_LOOKBACK:
                zt = jnp.where(diag, zt, NEG_INF)
            if b < SB_LOOKBACK:
                zt = jnp.where(is_first, NEG_INF, zt)
            logw, carry = _sb_tile(zt, tri2, carry)
            pieces.insert(0, logw)
        w = jnp.exp(jnp.concatenate(pieces, axis=1)).astype(BF16)
        accs.append(_dot(w, vw[:(a + SB_LOOKBACK + 1) * SB_BLOCK]))
        carries.append(carry)
    acc = jnp.concatenate(accs, axis=0)
    carry = jnp.concatenate(carries, axis=0)
    acc = _sb_older_blocks(q, k_ref, v_ref, tri2, base - SB_LOOKBACK - 1, acc, carry)
    o_ref[...] = acc.astype(o_ref.dtype)


def sb_prompt(q, k, v):
    s = q.shape[0]
    tq = min(SB_QUERY_ROWS, s)
    assert s % tq == 0 and tq % SB_BLOCK == 0 and s >= tq + SB_LOOKBACK * SB_BLOCK
    head_all = pl.BlockSpec((s, HEAD_DIM), lambda h, i: (0, h))
    return pl.pallas_call(
        _sb_prompt_kernel,
        grid=(HC, s // tq),
        in_specs=[pl.BlockSpec((tq, HEAD_DIM), lambda h, i: (i, h)), head_all, head_all],
        out_specs=pl.BlockSpec((tq, HEAD_DIM), lambda h, i: (i, h)),
        out_shape=jax.ShapeDtypeStruct(q.shape, BF16),
        compiler_params=_params(("parallel", "arbitrary")),
        name="sb_prompt",
    )(q, k, v)


def _sb_sample_kernel(q_ref, kn_ref, vn_ref, kc_ref, vc_ref, o_ref, acc_ref, carry_ref):
    s = pl.program_id(1)
    t = q_ref.shape[0]
    n_heads = acc_ref.shape[0]
    tri2 = _tri2()

    def head_q(h):
        return q_ref[:, h * HEAD_DIM:(h + 1) * HEAD_DIM].astype(BF16)

    @pl.when(s == 0)
    def _():
        zeros = jnp.zeros((SB_BLOCK - t, HEAD_DIM), BF16)
        mask = _causal_tile_mask(t)
        for h in range(n_heads):
            cols = slice(h * HEAD_DIM, (h + 1) * HEAD_DIM)
            kb = jnp.concatenate([kn_ref[:, cols].astype(BF16), zeros], axis=0)
            vb = jnp.concatenate([vn_ref[:, cols].astype(BF16), zeros], axis=0)
            logw, carry = _sb_tile(jnp.where(mask, _dot_t(head_q(h), kb), NEG_INF), tri2, None)
            acc_ref[h] = _dot(jnp.exp(logw).astype(BF16), vb)
            carry_ref[h] = carry

    @pl.when(jnp.min(carry_ref[...]) < SB_SKIP_LOG)
    def _():
        for h in range(n_heads):
            kb = kc_ref[:, h, :].astype(BF16)
            vb = vc_ref[:, h, :].astype(BF16)
            logw, carry = _sb_tile(_dot_t(head_q(h), kb), tri2, carry_ref[h])
            acc_ref[h] += _dot(jnp.exp(logw).astype(BF16), vb)
            carry_ref[h] = carry

    @pl.when(s == pl.num_programs(1) - 1)
    def _():
        for h in range(n_heads):
            o_ref[:, h * HEAD_DIM:(h + 1) * HEAD_DIM] = acc_ref[h].astype(o_ref.dtype)


def sb_sample(q, k_new, v_new, cache_k, cache_v, layer, *, t):
    _, nb, past, n_heads, _ = cache_k.shape
    assert past % SB_BLOCK == 0 and t <= SB_BLOCK
    nblk = past // SB_BLOCK
    rows = pl.BlockSpec((t, n_heads * HEAD_DIM), lambda b, s: (b, 0))
    cache_spec = pl.BlockSpec((None, None, SB_BLOCK, n_heads, HEAD_DIM), lambda b, s: (layer, b, nblk - 1 - s, 0, 0))
    return pl.pallas_call(
        _sb_sample_kernel,
        grid=(nb, nblk),
        in_specs=[rows, rows, rows, cache_spec, cache_spec],
        out_specs=rows,
        out_shape=jax.ShapeDtypeStruct(q.shape, BF16),
        scratch_shapes=[pltpu.VMEM((n_heads, t, HEAD_DIM), F32), pltpu.VMEM((n_heads, t, HEAD_DIM), F32)],
        compiler_params=_params(("parallel", "arbitrary")),
        name="sb_sample",
    )(q, k_new, v_new, cache_k, cache_v)


def _rope_table_kernel(inv_ref, cos_ref, sin_ref, *, pos0):
    tm = cos_ref.shape[0]
    pos = lax.broadcasted_iota(jnp.int32, (tm, HEAD_DIM), 0) + (pos0 + pl.program_id(0) * tm)
    lane = lax.broadcasted_iota(jnp.int32, (tm, HEAD_DIM), 1)
    ang = pos.astype(F32) * inv_ref[...]
    sin = jnp.sin(ang)
    cos_ref[...] = jnp.cos(ang)
    sin_ref[...] = jnp.where(lane < HEAD_DIM // 2, -sin, sin)


def _rope_tables(n, pos0):
    half = HEAD_DIM // 2
    inv = ROPE_THETA ** (-jnp.arange(half, dtype=F32) / half)
    inv = jnp.concatenate([inv, inv]).reshape(1, HEAD_DIM)
    tm = min(1024, n)
    assert n % tm == 0
    return pl.pallas_call(
        functools.partial(_rope_table_kernel, pos0=pos0),
        grid=(n // tm,),
        in_specs=[pl.BlockSpec((1, HEAD_DIM), lambda i: (0, 0))],
        out_specs=[pl.BlockSpec((tm, HEAD_DIM), lambda i: (i, 0))] * 2,
        out_shape=[jax.ShapeDtypeStruct((n, HEAD_DIM), F32)] * 2,
        compiler_params=_params(("parallel",)),
        name="rope_tables",
    )(inv)


def kernel(x_prompt, x_sample, cache_a_k, cache_a_v, cache_b_k, cache_b_v, cache_c_k, cache_c_v, norm_mix, w_in_ab, sink_a, rel_bias_b, w_out_ab, w_in_c, w_out_c, norm_ffn, w_gate, w_up, w_down, norm_final):
    bp, seq, d = x_prompt.shape
    nb, t, _ = x_sample.shape
    depth = norm_mix.shape[0]
    past = cache_c_k.shape[2]
    assert bp == 1 and seq >= B_REACH
    keep_a, keep_b = min(A_REACH, seq), min(B_REACH, seq)
    wc = HC * HEAD_DIM

    yp = x_prompt.reshape(seq, d)
    ys = x_sample.reshape(nb * t, d)
    cos_p, sin_p = _rope_tables(seq, 0)
    cos_s, sin_s = _rope_tables(t, past)
    cos_s, sin_s = jnp.tile(cos_s, (nb, 1)), jnp.tile(sin_s, (nb, 1))

    outs = {name: [] for name in ("pa_k", "pa_v", "pb_k", "pb_v", "sc_k", "sc_v")}
    n_c = cache_c_k.shape[0]
    sa_k = sa_v = sb_k = sb_v = pc_k = pc_v = None
    hp, qp = norm_inputs(yp)
    hs, qs_ = norm_inputs(ys)

    def weight_jobs(layer):
        i = layer // 2
        if layer % 2 == 0:
            mixer = [CastJob(w_in_ab, i, norm_mix[layer], AB_TILE), CastJob(w_out_ab, i, None, OUT_TILE)]
        else:
            mixer = [CastJob(w_in_c, i, norm_mix[layer], None), CastJob(w_out_c, i, None, None)]
        return mixer + [CastJob(w_gate, layer, norm_ffn[layer], FF_TILE), CastJob(w_up, layer, norm_ffn[layer], FF_TILE)]

    w_in, w_out, wg, wu = [cast_bf16(*job) for job in weight_jobs(0)]
    for layer in range(depth):
        i = layer // 2
        if layer % 2 == 0:
            bias = bias_prep(rel_bias_b[i])
            proj = matmul_ab(hp, qp, w_in, cos_p, sin_p, BF16)
            oa = band_prompt(proj, q_col=0, k_col=AB_SPLITS[0], v_col=AB_SPLITS[1], n_kv=KV_A, group=G_A,
                             halo=A_REACH, n_band=A_BAND_CHUNKS, sink=sink_a[i])
            ob = band_prompt(proj, q_col=AB_SPLITS[2], k_col=AB_SPLITS[3], v_col=AB_SPLITS[4], n_kv=HB, group=1,
                             halo=B_REACH, n_band=B_BAND_CHUNKS, bias=bias)
            tail = matmul_ab(hp[seq - keep_b:], qp[seq - keep_b:], w_in, cos_p[seq - keep_b:], sin_p[seq - keep_b:], F32)
            yp, hp, qp = matmul2_res(oa, ob, w_out, yp)
            outs["pa_k"].append(tail[keep_b - keep_a:, AB_SPLITS[0]:AB_SPLITS[1]].reshape(1, keep_a, KV_A, HEAD_DIM))
            outs["pa_v"].append(tail[keep_b - keep_a:, AB_SPLITS[1]:AB_SPLITS[2]].reshape(1, keep_a, KV_A, HEAD_DIM))
            outs["pb_k"].append(tail[:, AB_SPLITS[3]:AB_SPLITS[4]].reshape(1, keep_b, HB, HEAD_DIM))
            outs["pb_v"].append(tail[:, AB_SPLITS[4]:].reshape(1, keep_b, HB, HEAD_DIM))
            projs = matmul_ab(hs, qs_, w_in, cos_s, sin_s, F32)
            qa, ka, va, qb, kb, vb = jnp.split(projs, AB_SPLITS, axis=1)
            oa, sa_k, sa_v = band_sample(qa, ka, va, cache_a_k, cache_a_v, sa_k, sa_v, i, t=t, group=G_A,
                                         sink=sink_a[i])
            ob, sb_k, sb_v = band_sample(qb, kb, vb, cache_b_k, cache_b_v, sb_k, sb_v, i, t=t, group=1, bias=bias)
            ys, hs, qs_ = matmul2_res(oa, ob, w_out, ys)
        else:
            q = matmul(hp, qp, w_in, BF16, scale=QK_SCALE, col0=0, n=wc)
            pc_k, k16 = matmul_heads(hp, qp, w_in, pc_k, i, n_c, col0=wc, n=wc)
            pc_v, v16 = matmul_heads(hp, qp, w_in, pc_v, i, n_c, col0=2 * wc, n=wc)
            att = sb_prompt(q, k16, v16)
            yp, hp, qp = matmul_res(att, w_out, yp, tm=1024, tn=512)
            q = matmul(hs, qs_, w_in, F32, scale=QK_SCALE, col0=0, n=wc)
            ks = matmul(hs, qs_, w_in, F32, col0=wc, n=wc)
            vs = matmul(hs, qs_, w_in, F32, col0=2 * wc, n=wc)
            att = sb_sample(q, ks, vs, cache_c_k, cache_c_v, i, t=t)
            ys, hs, qs_ = matmul_res(att, w_out, ys, tm=1024, tn=512)
            outs["sc_k"].append(ks.reshape(nb, t, HC, HEAD_DIM))
            outs["sc_v"].append(vs.reshape(nb, t, HC, HEAD_DIM))
        jobs = [CastJob(w_down, layer, None, OUT_TILE)] + (weight_jobs(layer + 1) if layer + 1 < depth else [])
        act_p, casts = gateup(hp, qp, wg, wu, jobs)
        act_s, _ = gateup(hs, qs_, wg, wu)
        wd = casts[0]
        yp, hp, qp = matmul_res(act_p, wd, yp)
        ys, hs, qs_ = matmul_res(act_s, wd, ys)
        if layer + 1 < depth:
            w_in, w_out, wg, wu = casts[1:]

    y_prompt = rmsnorm(yp, norm_final, F32).reshape(1, seq, d)
    y_sample = rmsnorm(ys, norm_final, F32).reshape(nb, t, d)
    st = {name: jnp.stack(v) for name, v in outs.items()}
    pc_k = pc_k.reshape(n_c, 1, seq, HC, HEAD_DIM)
    pc_v = pc_v.reshape(n_c, 1, seq, HC, HEAD_DIM)
    return (y_prompt, y_sample, st["pa_k"], st["pa_v"], st["pb_k"], st["pb_v"], pc_k, pc_v,
            sa_k, sa_v, sb_k, sb_v, st["sc_k"], st["sc_v"])
```
